```python
import math
import functools
import jax
import jax.numpy as jnp
from jax import lax
import numpy as np

D_MODEL = 4096
BATCH = 4
SEQ = 2048
DEPTH = 2
DEC_BATCH = 8
DEC_SEQ = 1
PAST_LEN = 16384
PAGE_SIZE = 128

N_HEADS = 16
HEAD_DIM = 128
ATT_WIDTH = N_HEADS * HEAD_DIM
IDX_HEADS = 32
IDX_DIM = 128
TOPK_MAX = 256
Q_BLOCK = 128
N_BUCKETS = 32
MAX_DISTANCE = 128
D_RNN = 2048
RNN_BLOCKS = 16
RNN_BLOCK_W = D_RNN // RNN_BLOCKS
CONV_W = 4
RG_C = 8.0
D_FF = 4 * D_MODEL
NORM_EPS = 1e-6
SPLIT_SIZES = (ATT_WIDTH, ATT_WIDTH, ATT_WIDTH, IDX_HEADS * IDX_DIM, IDX_DIM, IDX_HEADS, D_RNN, D_RNN, D_MODEL, D_MODEL)
IN_WIDTH = 3 * ATT_WIDTH + IDX_HEADS * IDX_DIM + IDX_DIM + IDX_HEADS + 2 * D_RNN + 2 * D_MODEL

kernel_name = 'dsa_rglru_gated_hybrid_step'


def rmsnorm(x, g):
    x32 = x.astype(jnp.float32)
    y = x32 * lax.rsqrt(jnp.mean(x32 * x32, axis=-1, keepdims=True) + NORM_EPS)
    return (y * g.astype(jnp.float32)).astype(x.dtype)


def split_in(p):
    outs = []
    start = 0
    for size in SPLIT_SIZES:
        outs.append(p[..., start:start + size])
        start += size
    return outs


def t5_bucket(dist):
    max_exact = N_BUCKETS // 2
    d = jnp.maximum(dist, 0)
    d_f = jnp.maximum(d, 1).astype(jnp.float32)
    large = max_exact + (jnp.log(d_f / max_exact) / math.log(MAX_DISTANCE / max_exact)
                         * (N_BUCKETS - max_exact)).astype(jnp.int32)
    large = jnp.minimum(large, N_BUCKETS - 1)
    return jnp.where(d < max_exact, d, large)


def indexer_scores(iq, iw, ik):
    s = jnp.einsum('...thd,...sd->...ths', iq, ik, preferred_element_type=jnp.float32) * IDX_DIM ** -0.5
    w = iw.astype(jnp.float32) * IDX_HEADS ** -0.5
    return jnp.einsum('...ths,...th->...ts', jax.nn.relu(s), w)


def select_topk(scores, q_pos, topk):
    n_keys = scores.shape[-1]
    visible = jnp.arange(n_keys, dtype=jnp.int32)[None, :] <= q_pos[:, None]
    masked = jnp.where(visible, scores, -jnp.inf)
    _, idx = lax.top_k(masked, topk)
    valid = idx <= q_pos[:, None]
    return idx, valid


def sparse_attend(q, k_sel, v_sel, idx, valid, q_pos, rel_table):
    bias = rel_table.astype(jnp.float32)[t5_bucket(q_pos[:, None] - idx)]
    logits = jnp.einsum('...thd,...tkhd->...thk', q, k_sel, preferred_element_type=jnp.float32) * HEAD_DIM ** -0.5
    logits = logits + jnp.swapaxes(bias, -1, -2)
    logits = jnp.where(valid[..., None, :], logits, -jnp.inf)
    p = jax.nn.softmax(logits, axis=-1)
    return jnp.einsum('...thk,...tkhd->...thd', p, v_sel.astype(jnp.float32)).astype(q.dtype)


def dsa_prompt(q, k, v, iq, iw, ik, rel_table):
    B, T = q.shape[:2]
    topk = min(TOPK_MAX, T // 4)
    nblk = T // Q_BLOCK

    def blocks(a):
        return a.reshape((B * nblk, Q_BLOCK) + a.shape[2:])

    b_ids = jnp.repeat(jnp.arange(B, dtype=jnp.int32), nblk)
    t0s = jnp.tile(jnp.arange(nblk, dtype=jnp.int32) * Q_BLOCK, B)

    def one(args):
        qb, iqb, iwb, b, t0 = args
        q_pos = t0 + jnp.arange(Q_BLOCK, dtype=jnp.int32)
        idx, valid = select_topk(indexer_scores(iqb, iwb, ik[b]), q_pos, topk)
        k_sel = jnp.take(k[b], idx, axis=0)
        v_sel = jnp.take(v[b], idx, axis=0)
        return sparse_attend(qb, k_sel, v_sel, idx, valid, q_pos, rel_table)

    out = lax.map(one, (blocks(q), blocks(iq), blocks(iw), b_ids, t0s))
    return out.reshape(B, T, N_HEADS, HEAD_DIM)


def dsa_sample(q, k_new, v_new, iq, iw, ik_new, cache_k, cache_v, cache_idx_k, layer, page_table, rel_table):
    DB, Tn = q.shape[:2]
    n_pages = page_table.shape[1]
    past = n_pages * PAGE_SIZE
    topk = min(TOPK_MAX, (past + Tn) // 4)
    ik_past = cache_idx_k[layer, page_table].reshape(DB, past, IDX_DIM).astype(ik_new.dtype)
    ik_all = jnp.concatenate([ik_past, ik_new], axis=1)
    q_pos = past + jnp.arange(Tn, dtype=jnp.int32)
    idx, valid = select_topk(indexer_scores(iq, iw, ik_all), q_pos, topk)
    b_ix = jnp.arange(DB, dtype=jnp.int32)[:, None, None]
    in_past = (idx < past)[..., None, None]
    p_idx = jnp.minimum(idx, past - 1)
    phys = page_table[b_ix, p_idx // PAGE_SIZE]
    off = p_idx % PAGE_SIZE
    n_idx = jnp.clip(idx - past, 0, Tn - 1)
    k_sel = jnp.where(in_past, cache_k[layer, phys, off].astype(k_new.dtype), k_new[b_ix, n_idx])
    v_sel = jnp.where(in_past, cache_v[layer, phys, off].astype(v_new.dtype), v_new[b_ix, n_idx])
    return sparse_attend(q, k_sel, v_sel, idx, valid, q_pos, rel_table)


def rglru_branch(xr, xg, conv_buf, h0, conv_w, conv_b, w_a, b_a, w_x, b_x, lam):
    B, T, _ = xr.shape
    x_ext = jnp.concatenate([conv_buf.astype(xr.dtype), xr], axis=1)
    xc = conv_b + x_ext[:, 0:T] * conv_w[0]
    for j in range(1, CONV_W):
        xc = xc + x_ext[:, j:j + T] * conv_w[j]
    xb = xc.reshape(B, T, RNN_BLOCKS, RNN_BLOCK_W)
    r = jax.nn.sigmoid(jnp.einsum('btnd,nde->btne', xb, w_a, preferred_element_type=jnp.float32).reshape(B, T, D_RNN) + b_a)
    i = jax.nn.sigmoid(jnp.einsum('btnd,nde->btne', xb, w_x, preferred_element_type=jnp.float32).reshape(B, T, D_RNN) + b_x)
    log_a = -RG_C * r * jax.nn.softplus(-lam.astype(jnp.float32))
    u = jnp.sqrt(-jnp.expm1(2.0 * log_a)) * i * xc.astype(jnp.float32)

    def step(h, a_u):
        a_t, u_t = a_u
        h = a_t * h + u_t
        return h, h

    h_last, hs = lax.scan(step, h0.astype(jnp.float32),
                          (jnp.swapaxes(jnp.exp(log_a), 0, 1), jnp.swapaxes(u, 0, 1)))
    y = jnp.swapaxes(hs, 0, 1) * jax.nn.gelu(xg.astype(jnp.float32))
    return y.astype(xr.dtype), h_last.astype(xr.dtype), x_ext[:, T:]


def run_layer(x, c, attend, conv_buf, h0, w_ada_l, b_ada_l, g_l, w_in_l, conv_w_l, conv_b_l,
              w_rg_a_l, b_rg_a_l, w_rg_x_l, b_rg_x_l, lam_l, w_up_att_l, w_up_rnn_l, w_o_l,
              w_mlp_in_l, w_mlp_out_l):
    B, T, _ = x.shape
    mod = jax.nn.silu(c) @ w_ada_l + b_ada_l
    sh1, sc1, gt1, sh2, sc2, gt2 = [m[:, None, :] for m in jnp.split(mod, 6, axis=-1)]
    h = rmsnorm(x, g_l[0]) * (1 + sc1) + sh1
    q, k, v, iq, ik, iw, xr, xg, ga, gr = split_in(h @ w_in_l)
    q = q.reshape(B, T, N_HEADS, HEAD_DIM)
    k = k.reshape(B, T, N_HEADS, HEAD_DIM)
    v = v.reshape(B, T, N_HEADS, HEAD_DIM)
    iq = iq.reshape(B, T, IDX_HEADS, IDX_DIM)
    y_att = attend(q, k, v, iq, iw, ik)
    y_rnn, h_last, new_buf = rglru_branch(xr, xg, conv_buf, h0, conv_w_l, conv_b_l,
                                          w_rg_a_l, b_rg_a_l, w_rg_x_l, b_rg_x_l, lam_l)
    merged = (jax.nn.sigmoid(ga) * (y_att.reshape(B, T, ATT_WIDTH) @ w_up_att_l)
              + jax.nn.sigmoid(gr) * (y_rnn @ w_up_rnn_l))
    x = x + gt1 * rmsnorm(merged @ w_o_l, g_l[1])
    h = rmsnorm(x, g_l[2]) * (1 + sc2) + sh2
    ff = jnp.square(jax.nn.relu(h @ w_mlp_in_l)) @ w_mlp_out_l
    x = x + gt2 * rmsnorm(ff, g_l[3])
    return x, k, v, ik, h_last, new_buf


def setup_inputs(seed: int = 0) -> dict:
    key = jax.random.key(seed)
    keys = jax.random.split(key, 30)
    f32 = jnp.float32

    def nrm(k, shape, s):
        return jax.random.normal(k, shape, f32) * s

    n_pages = PAST_LEN // PAGE_SIZE
    n_used = DEC_BATCH * n_pages
    n_pool = n_used + max(1, n_used // 4)
    page_table = jax.random.permutation(keys[0], n_pool)[:n_used].reshape(DEC_BATCH, n_pages).astype(jnp.int32)
    a_c = jax.random.uniform(keys[1], (DEPTH, D_RNN), f32, 0.9, 0.999)
    a = a_c ** (1.0 / RG_C)
    rg_lambda = jnp.log(a) - jnp.log1p(-a)
    return {
        'x_prompt': nrm(keys[2], (BATCH, SEQ, D_MODEL), 1.0),
        'x_sample': nrm(keys[3], (DEC_BATCH, DEC_SEQ, D_MODEL), 1.0),
        'cache_k': nrm(keys[4], (DEPTH, n_pool, PAGE_SIZE, N_HEADS, HEAD_DIM), 1.0),
        'cache_v': nrm(keys[5], (DEPTH, n_pool, PAGE_SIZE, N_HEADS, HEAD_DIM), 1.0),
        'cache_idx_k': nrm(keys[6], (DEPTH, n_pool, PAGE_SIZE, IDX_DIM), 1.0),
        'state_rglru_h': nrm(keys[7], (DEPTH, DEC_BATCH, D_RNN), 0.5),
        'state_conv': nrm(keys[8], (DEPTH, DEC_BATCH, CONV_W - 1, D_RNN), 1.0),
        'page_table': page_table,
        'c_prompt': nrm(keys[9], (BATCH, D_MODEL), 1.0),
        'c_sample': nrm(keys[10], (DEC_BATCH, D_MODEL), 1.0),
        'w_ada': nrm(keys[11], (DEPTH, D_MODEL, 6 * D_MODEL), 0.2 * D_MODEL ** -0.5),
        'b_ada': nrm(keys[12], (DEPTH, 6 * D_MODEL), 0.02),
        'norm_g': 1.0 + nrm(keys[13], (DEPTH, 4, D_MODEL), 0.02),
        'w_in': nrm(keys[14], (DEPTH, D_MODEL, IN_WIDTH), D_MODEL ** -0.5),
        'rel_bias': nrm(keys[15], (N_BUCKETS, N_HEADS), 0.2),
        'conv_w': nrm(keys[16], (DEPTH, CONV_W, D_RNN), CONV_W ** -0.5),
        'conv_b': nrm(keys[17], (DEPTH, D_RNN), 0.01),
        'w_rg_a': nrm(keys[18], (DEPTH, RNN_BLOCKS, RNN_BLOCK_W, RNN_BLOCK_W), RNN_BLOCK_W ** -0.5),
        'b_rg_a': nrm(keys[19], (DEPTH, D_RNN), 0.01),
        'w_rg_x': nrm(keys[20], (DEPTH, RNN_BLOCKS, RNN_BLOCK_W, RNN_BLOCK_W), RNN_BLOCK_W ** -0.5),
        'b_rg_x': nrm(keys[21], (DEPTH, D_RNN), 0.01),
        'rg_lambda': rg_lambda,
        'w_up_att': nrm(keys[22], (DEPTH, ATT_WIDTH, D_MODEL), ATT_WIDTH ** -0.5),
        'w_up_rnn': nrm(keys[23], (DEPTH, D_RNN, D_MODEL), D_RNN ** -0.5),
        'w_o': nrm(keys[24], (DEPTH, D_MODEL, D_MODEL), D_MODEL ** -0.5),
        'w_mlp_in': nrm(keys[25], (DEPTH, D_MODEL, D_FF), D_MODEL ** -0.5),
        'w_mlp_out': nrm(keys[26], (DEPTH, D_FF, D_MODEL), D_FF ** -0.5),
    }


def reference(x_prompt, x_sample, cache_k, cache_v, cache_idx_k, state_rglru_h, state_conv, page_table,
              c_prompt, c_sample, w_ada, b_ada, norm_g, w_in, rel_bias, conv_w, conv_b, w_rg_a, b_rg_a,
              w_rg_x, b_rg_x, rg_lambda, w_up_att, w_up_rnn, w_o, w_mlp_in, w_mlp_out):
    xp, xs = x_prompt, x_sample
    zero_buf = jnp.zeros((xp.shape[0], CONV_W - 1, D_RNN), xp.dtype)
    zero_h = jnp.zeros((xp.shape[0], D_RNN), jnp.float32)
    attend_p = functools.partial(dsa_prompt, rel_table=rel_bias)
    kp, vp, ikp, hp, cp = [], [], [], [], []
    ks, vs, iks, hs, cs = [], [], [], [], []
    for l in range(DEPTH):
        lw = (w_ada[l], b_ada[l], norm_g[l], w_in[l], conv_w[l], conv_b[l], w_rg_a[l], b_rg_a[l],
              w_rg_x[l], b_rg_x[l], rg_lambda[l], w_up_att[l], w_up_rnn[l], w_o[l], w_mlp_in[l], w_mlp_out[l])
        attend_s = functools.partial(dsa_sample, cache_k=cache_k, cache_v=cache_v, cache_idx_k=cache_idx_k,
                                     layer=l, page_table=page_table, rel_table=rel_bias)
        xp, k_, v_, ik_, h_, b_ = run_layer(xp, c_prompt, attend_p, zero_buf, zero_h, *lw)
        kp.append(k_)
        vp.append(v_)
        ikp.append(ik_)
        hp.append(h_)
        cp.append(b_)
        xs, k_, v_, ik_, h_, b_ = run_layer(xs, c_sample, attend_s, state_conv[l], state_rglru_h[l], *lw)
        ks.append(k_)
        vs.append(v_)
        iks.append(ik_)
        hs.append(h_)
        cs.append(b_)
    return (xp, xs, jnp.stack(kp), jnp.stack(vp), jnp.stack(ikp), jnp.stack(hp), jnp.stack(cp),
            jnp.stack(ks), jnp.stack(vs), jnp.stack(iks), jnp.stack(hs), jnp.stack(cs))
```

```python
import functools
import math

import jax
import jax.numpy as jnp
from jax import lax
from jax.experimental import pallas as pl
from jax.experimental.pallas import tpu as pltpu

F32 = jnp.float32
BF16 = jnp.bfloat16
I32 = jnp.int32

V7X_VMEM_BYTES = 64 * 1024 * 1024
VMEM_LIMIT = V7X_VMEM_BYTES - 8 * 1024 * 1024
LANES = 128
SUBLANES = 8

TOPK_MAX = 256
MAX_DISTANCE = 128
RG_C = 8.0
NORM_EPS = 1e-6
CONV_W = 4
MASK_NEG = -1e30
INT_MIN = -2 ** 31
INT_MAX = 2 ** 31 - 1


def _params(n_axes):
    return pltpu.CompilerParams(dimension_semantics=("arbitrary",) * n_axes, vmem_limit_bytes=VMEM_LIMIT)


def _pick_tile(n, col0, candidates=(512, 256, 128)):
    for t in candidates:
        if n % t == 0 and col0 % t == 0:
            return t
    raise ValueError(f"no lane tile for width {n} at column {col0}")


def _mm_body(*refs, n_extra, n_out, cast_w, a_fn, epilogue):
    a_ref, w_ref = refs[0], refs[1]
    extra = refs[2:2 + n_extra]
    outs = refs[2 + n_extra:2 + n_extra + n_out]
    if cast_w:
        wb_ref = refs[2 + n_extra + n_out]

        @pl.when(pl.program_id(1) == 0)
        def _():
            wb_ref[...] = w_ref[...].astype(BF16)

        w = wb_ref[...]
    else:
        w = w_ref[...]
    a = a_ref[...]
    if a_fn is not None:
        a = a_fn(a)
    acc = jnp.dot(a, w, preferred_element_type=F32)
    res = epilogue(acc, *[e[...] for e in extra])
    for o_ref, r in zip(outs, res):
        o_ref[...] = r.astype(o_ref.dtype)


def _matmul(a, w, layer, col0, n, outs, epilogue, *, extras=(), a_fn=None, name="mm"):
    M, K = a.shape
    tm = min(M, 1024)
    tn = _pick_tile(n, col0)
    assert M % tm == 0
    cast_w = w.dtype != BF16
    cb = col0 // tn
    in_specs = [pl.BlockSpec((tm, K), lambda j, i: (i, 0)),
                pl.BlockSpec((None, K, tn), lambda j, i: (layer, 0, cb + j))]
    in_specs += [spec for _, spec in extras]
    out_specs = [pl.BlockSpec((tm, tn * wd // n), lambda j, i: (i, j)) for wd, _ in outs]
    out_shape = [jax.ShapeDtypeStruct((M, wd), dt) for wd, dt in outs]
    body = functools.partial(_mm_body, n_extra=len(extras), n_out=len(outs), cast_w=cast_w,
                             a_fn=a_fn, epilogue=epilogue)
    return pl.pallas_call(
        body, grid=(n // tn, M // tm), in_specs=in_specs, out_specs=out_specs, out_shape=out_shape,
        scratch_shapes=[pltpu.VMEM((K, tn), BF16)] if cast_w else [],
        compiler_params=_params(2), name=name,
    )(a, w, *[arr for arr, _ in extras])


def _rms(x, g):
    return x * lax.rsqrt(jnp.mean(x * x, axis=-1, keepdims=True) + NORM_EPS) * g


def _prenorm_body(x_ref, g_ref, sh_ref, sc_ref, h_ref):
    h = _rms(x_ref[...], g_ref[...]) * (1.0 + sc_ref[...]) + sh_ref[...]
    h_ref[...] = h.astype(h_ref.dtype)


def _postnorm_body(*refs, with_next):
    if with_next:
        x_ref, y_ref, gpost_ref, gate_ref, gpre_ref, sh_ref, sc_ref, xo_ref, h_ref = refs
    else:
        x_ref, y_ref, gpost_ref, gate_ref, xo_ref = refs
    x = x_ref[...] + gate_ref[...] * _rms(y_ref[...], gpost_ref[...])
    xo_ref[...] = x
    if with_next:
        h = _rms(x, gpre_ref[...]) * (1.0 + sc_ref[...]) + sh_ref[...]
        h_ref[...] = h.astype(h_ref.dtype)


class _Stream:
    def __init__(self, mod, rows_per_seq, n_rows, d):
        self.mod = mod
        self.rows_per_seq = rows_per_seq
        self.tm = min(256, n_rows)
        self.d = d

    def mod_spec(self, chunk):
        tm, d = self.tm, self.d
        if self.rows_per_seq > 1:
            rps = self.rows_per_seq
            return pl.BlockSpec((None, 1, d), lambda i: (i * tm // rps, 0, chunk))
        return pl.BlockSpec((tm, d), lambda i: (i, chunk))


def _norm_spec(g_index, d):
    return pl.BlockSpec((None, 1, d), lambda i: (g_index, 0, 0))


def _prenorm(x, norm_g3, g_index, st, shift_chunk, scale_chunk):
    M, D = x.shape
    tm = st.tm
    row = pl.BlockSpec((tm, D), lambda i: (i, 0))
    return pl.pallas_call(
        _prenorm_body, grid=(M // tm,),
        in_specs=[row, _norm_spec(g_index, D), st.mod_spec(shift_chunk), st.mod_spec(scale_chunk)],
        out_specs=row, out_shape=jax.ShapeDtypeStruct((M, D), BF16),
        compiler_params=_params(1), name="prenorm",
    )(x, norm_g3, st.mod, st.mod)


def _postnorm(x, y, norm_g3, g_post, st, gate_chunk, nxt=None):
    M, D = x.shape
    tm = st.tm
    row = pl.BlockSpec((tm, D), lambda i: (i, 0))
    in_specs = [row, row, _norm_spec(g_post, D), st.mod_spec(gate_chunk)]
    args = [x, y, norm_g3, st.mod]
    out_specs = [row]
    out_shape = [jax.ShapeDtypeStruct((M, D), F32)]
    if nxt is not None:
        g_pre, st_n, sh_c, sc_c = nxt
        in_specs += [_norm_spec(g_pre, D), st_n.mod_spec(sh_c), st_n.mod_spec(sc_c)]
        args += [norm_g3, st_n.mod, st_n.mod]
        out_specs.append(row)
        out_shape.append(jax.ShapeDtypeStruct((M, D), BF16))
    res = pl.pallas_call(
        functools.partial(_postnorm_body, with_next=nxt is not None), grid=(M // tm,),
        in_specs=in_specs, out_specs=out_specs, out_shape=out_shape,
        compiler_params=_params(1), name="postnorm",
    )(*args)
    return res if nxt is not None else (res[0], None)


def _t5_bucket(dist, n_buckets):
    max_exact = n_buckets // 2
    d = jnp.maximum(dist, 0)
    d_f = jnp.maximum(d, 1).astype(F32)
    large = max_exact + (jnp.log(d_f / max_exact) / math.log(MAX_DISTANCE / max_exact)
                         * (n_buckets - max_exact)).astype(I32)
    large = jnp.minimum(large, n_buckets - 1)
    return jnp.where(d < max_exact, d, large)


def _bias_tiles_body(rel_ref, o_ref, *, tq, n_buckets):
    which = pl.program_id(0)
    h = pl.program_id(1)
    r = lax.broadcasted_iota(I32, (tq, tq), 0)
    c = lax.broadcasted_iota(I32, (tq, tq), 1)
    bucket = _t5_bucket(which * tq + c - r, n_buckets)
    far = rel_ref[n_buckets - 1, h]
    acc = jnp.zeros((tq, tq), F32)
    for n in range(n_buckets - 1):
        acc = jnp.where(bucket == n, rel_ref[n, h] - far, acc)
    o_ref[...] = acc


def _bias_tiles(rel_bias, tq):
    nb, n_heads = rel_bias.shape
    return pl.pallas_call(
        functools.partial(_bias_tiles_body, tq=tq, n_buckets=nb), grid=(2, n_heads),
        in_specs=[pl.BlockSpec(memory_space=pltpu.SMEM)],
        out_specs=pl.BlockSpec((None, None, tq, tq), lambda w, h: (w, h, 0, 0)),
        out_shape=jax.ShapeDtypeStruct((2, n_heads, tq, tq), F32),
        compiler_params=_params(2), name="bias_tiles",
    )(rel_bias)


def _attn_body(qT_ref, k_ref, vT_ref, iqT_ref, ik_ref, wT_ref, bias_ref, o_ref,
               sc_ref, key_ref, mask_ref, p_ref, *, topk, n_heads, head_dim, idx_heads, idx_dim, tq, n_chunks):
    i = pl.program_id(1)
    nck = i + 1
    w_scale = idx_heads ** -0.5

    def rows(c):
        return pl.ds(pl.multiple_of(c * tq, tq), tq)

    for h in range(idx_heads):
        iq_h = iqT_ref[h * idx_dim:(h + 1) * idx_dim, :]
        w_h = wT_ref[h:h + 1, :] * w_scale

        def score_chunk(c, carry, iq_h=iq_h, w_h=w_h, first=(h == 0)):
            s = jnp.dot(ik_ref[rows(c), :], iq_h, preferred_element_type=F32)
            contrib = jnp.maximum(s, 0.0) * w_h
            if first:
                sc_ref[rows(c), :] = contrib
            else:
                sc_ref[rows(c), :] += contrib
            return carry

        lax.fori_loop(0, nck, score_chunk, 0)

    q_pos = i * tq + lax.broadcasted_iota(I32, (tq, tq), 1)
    r_loc = lax.broadcasted_iota(I32, (tq, tq), 0)

    def key_chunk(c, carry):
        bits = pltpu.bitcast(sc_ref[rows(c), :] + 0.0, I32)
        key = jnp.where(bits < 0, bits ^ INT_MAX, bits)
        key_ref[rows(c), :] = jnp.where(c * tq + r_loc <= q_pos, key, INT_MIN)
        return carry

    lax.fori_loop(0, nck, key_chunk, 0)

    def count(pred):
        def body(c, cnt):
            hit = pred(key_ref[rows(c), :], c).astype(I32).reshape(tq // SUBLANES, SUBLANES, tq)
            return cnt + jnp.sum(hit, axis=0)
        cnt = lax.fori_loop(0, nck, body, jnp.zeros((SUBLANES, tq), I32))
        return jnp.sum(cnt, axis=0, keepdims=True)

    thr = jnp.where(count(lambda k, c: k >= 0) >= topk, 0, INT_MIN).astype(I32)

    def thr_bit(s, thr):
        cand = thr | jnp.left_shift(jnp.int32(1), 30 - s)
        return jnp.where(count(lambda k, c: k >= cand) >= topk, cand, thr)

    thr = lax.fori_loop(0, 31, thr_bit, thr)

    n_gt = count(lambda k, c: k > thr)
    n_ge = count(lambda k, c: k >= thr)
    need = topk - n_gt
    p_ref[...] = jnp.full((1, tq), INT_MAX, I32)
    has_ties = jnp.max(jnp.where((n_ge > topk) & (thr > INT_MIN), 1, 0)) > 0

    @pl.when(has_ties)
    def _():
        n_bits = max(1, (n_chunks * tq - 1).bit_length())

        def pos_bit(s, p):
            cand = p | jnp.left_shift(jnp.int32(1), n_bits - 1 - s)
            below = count(lambda k, c: (k == thr) & (c * tq + r_loc < cand))
            return jnp.where(below < need, cand, p)

        p_ref[...] = lax.fori_loop(0, n_bits, pos_bit, jnp.zeros((1, tq), I32))

    p_last = p_ref[...]

    def mask_chunk(c, carry):
        k = key_ref[rows(c), :]
        pos = c * tq + r_loc
        keep = ((k > thr) | ((k == thr) & (pos <= p_last))) & (pos <= q_pos)
        mask_ref[c] = jnp.where(keep, 0.0, MASK_NEG)
        return carry

    lax.fori_loop(0, nck, mask_chunk, 0)

    c_prev = jnp.maximum(i - 1, 0)
    for h in range(n_heads):
        hs = slice(h * head_dim, (h + 1) * head_dim)
        q_h = qT_ref[hs, :]

        def attend(c, carry, bias=None, live=None, q_h=q_h, hs=hs):
            m, l, acc = carry
            s = jnp.dot(k_ref[rows(c), hs], q_h, preferred_element_type=F32) + mask_ref[c]
            if bias is not None:
                s = s + bias
            if live is not None:
                s = jnp.where(live, s, MASK_NEG)
            m_new = jnp.maximum(m, jnp.max(s, axis=0, keepdims=True))
            alpha = jnp.exp(m - m_new)
            p = jnp.exp(s - m_new)
            l = alpha * l + jnp.sum(p, axis=0, keepdims=True)
            acc = alpha * acc + jnp.dot(vT_ref[c, hs, :], p.astype(BF16), preferred_element_type=F32)
            return m_new, l, acc

        carry = (jnp.full((1, tq), MASK_NEG, F32), jnp.zeros((1, tq), F32), jnp.zeros((head_dim, tq), F32))
        carry = lax.fori_loop(0, c_prev, attend, carry)
        carry = attend(c_prev, carry, bias=bias_ref[1, h], live=i > 0)
        m, l, acc = attend(i, carry, bias=bias_ref[0, h])
        o_ref[hs, :] = (acc / l).astype(o_ref.dtype)


def _dsa_prompt(qT, kb, vT, iqT, ikb, wT, bias, *, B, T, tq, topk, n_heads, head_dim, idx_heads, idx_dim):
    nq = T // tq
    att = n_heads * head_dim
    once = dict(pipeline_mode=pl.Buffered(1))
    body = functools.partial(_attn_body, topk=topk, n_heads=n_heads, head_dim=head_dim,
                             idx_heads=idx_heads, idx_dim=idx_dim, tq=tq, n_chunks=nq)
    return pl.pallas_call(
        body, grid=(B, nq),
        in_specs=[
            pl.BlockSpec((None, att, tq), lambda b, i: (b * nq + i, 0, 0)),
            pl.BlockSpec((T, att), lambda b, i: (b, 0), **once),
            pl.BlockSpec((nq, att, tq), lambda b, i: (b, 0, 0), **once),
            pl.BlockSpec((None, idx_heads * idx_dim, tq), lambda b, i: (b * nq + i, 0, 0)),
            pl.BlockSpec((T, idx_dim), lambda b, i: (b, 0), **once),
            pl.BlockSpec((None, wT.shape[1], tq), lambda b, i: (b * nq + i, 0, 0)),
            pl.BlockSpec((2, n_heads, tq, tq), lambda b, i: (0, 0, 0, 0), **once),
        ],
        out_specs=pl.BlockSpec((None, att, tq), lambda b, i: (b * nq + i, 0, 0)),
        out_shape=jax.ShapeDtypeStruct((B * nq, att, tq), BF16),
        scratch_shapes=[pltpu.VMEM((T, tq), F32), pltpu.VMEM((T, tq), I32),
                        pltpu.VMEM((nq, tq, tq), F32), pltpu.VMEM((1, tq), I32)],
        compiler_params=_params(2), name="dsa_prompt",
    )(qT, kb, vT, iqT, ikb, wT, bias)


def _sample_scores_body(pt_ref, iq_ref, w_ref, page_ref, new_ref, o_ref, *, n_pages, idx_heads):
    j = pl.program_id(1)

    @pl.when(j == 0)
    def _():
        o_ref[...] = jnp.full(o_ref.shape, -jnp.inf, F32)

    def page_scores(keys):
        s = lax.dot_general(iq_ref[...], keys.astype(BF16), (((1,), (1,)), ((), ())),
                            preferred_element_type=F32)
        w = w_ref[...] * idx_heads ** -0.5
        return jnp.sum(jnp.maximum(s, 0.0) * w, axis=0, keepdims=True) + 0.0

    @pl.when(j < n_pages)
    def _():
        o_ref[pl.ds(j, 1), :] = page_scores(page_ref[...])

    @pl.when(j == n_pages)
    def _():
        sc = page_scores(new_ref[...])
        lane = lax.broadcasted_iota(I32, sc.shape, 1)
        o_ref[pl.ds(j, 1), :] = jnp.where(lane == 0, sc, -jnp.inf)


def _sample_scores(page_table, iq3, w3, cache_idx_k, new_pages, layer):
    DB, n_pages = page_table.shape
    _, idx_heads, idx_dim = iq3.shape
    page = cache_idx_k.shape[2]
    rows = n_pages + SUBLANES
    grid_spec = pltpu.PrefetchScalarGridSpec(
        num_scalar_prefetch=1, grid=(DB, n_pages + 1),
        in_specs=[
            pl.BlockSpec((None, idx_heads, idx_dim), lambda b, j, pt: (b, 0, 0)),
            pl.BlockSpec((None, idx_heads, page), lambda b, j, pt: (b, 0, 0)),
            pl.BlockSpec((None, None, page, idx_dim),
                         lambda b, j, pt: (layer, pt[b, jnp.minimum(j, n_pages - 1)], 0, 0)),
            pl.BlockSpec((None, page, idx_dim), lambda b, j, pt: (b, 0, 0)),
        ],
        out_specs=pl.BlockSpec((None, rows, page), lambda b, j, pt: (b, 0, 0)),
    )
    return pl.pallas_call(
        functools.partial(_sample_scores_body, n_pages=n_pages, idx_heads=idx_heads),
        grid_spec=grid_spec, out_shape=jax.ShapeDtypeStruct((DB, rows, page), F32),
        compiler_params=_params(2), name="sample_scores",
    )(page_table, iq3, w3, cache_idx_k, new_pages)


def _topk_body(s_ref, o_ref, work_ref, *, topk):
    work_ref[...] = s_ref[...]
    lane = lax.broadcasted_iota(I32, s_ref.shape, 1)
    slot = lax.broadcasted_iota(I32, o_ref.shape, 1)

    def body(k, acc):
        s = work_ref[...]
        m = jnp.max(s, axis=1, keepdims=True)
        idx = jnp.min(jnp.where(s == m, lane, INT_MAX), axis=1, keepdims=True)
        work_ref[...] = jnp.where(lane == idx, -jnp.inf, s)
        return jnp.where(slot == k, idx, acc)

    o_ref[...] = lax.fori_loop(0, topk, body, jnp.zeros(o_ref.shape, I32))


def _sample_topk(scores2d, topk):
    DB, n = scores2d.shape
    return pl.pallas_call(
        functools.partial(_topk_body, topk=topk),
        out_shape=jax.ShapeDtypeStruct((DB, topk), I32),
        scratch_shapes=[pltpu.VMEM((DB, n), F32)],
        compiler_params=pltpu.CompilerParams(vmem_limit_bytes=VMEM_LIMIT), name="sample_topk",
    )(scores2d)


def _sample_attend_body(idx_sm, pt_sm, q_ref, knew_ref, vnew_ref, idxc_ref, rel_ref, ck_hbm, cv_hbm, o_ref,
                        kbuf, vbuf, sem, *, layer, past, page, topk, n_heads, head_dim, n_buckets):
    b = pl.program_id(0)

    def row_copies(k):
        p = jnp.minimum(idx_sm[b, k], past - 1)
        phys = pt_sm[b, p // page]
        off = p % page
        dst = pl.ds(k, 1)
        return (pltpu.make_async_copy(ck_hbm.at[layer, phys, pl.ds(off, 1), :], kbuf.at[dst, :], sem.at[0]),
                pltpu.make_async_copy(cv_hbm.at[layer, phys, pl.ds(off, 1), :], vbuf.at[dst, :], sem.at[1]))

    def start(k, carry):
        for cp in row_copies(k):
            cp.start()
        return carry

    def wait(k, carry):
        for cp in row_copies(k):
            cp.wait()
        return carry

    lax.fori_loop(0, topk, start, 0)
    lax.fori_loop(0, topk, wait, 0)

    idxc = idxc_ref[...]
    is_new = idxc >= past
    ks = jnp.where(is_new, knew_ref[...], kbuf[...])
    vs = jnp.where(is_new, vnew_ref[...], vbuf[...])
    prod = ks * q_ref[...]
    bucket = _t5_bucket(past - idxc, n_buckets)
    bias = jnp.zeros((topk, LANES), F32)
    for n in range(n_buckets):
        bias = jnp.where(bucket == n, rel_ref[n:n + 1, :], bias)
    for h in range(n_heads):
        hs = slice(h * head_dim, (h + 1) * head_dim)
        s = jnp.sum(prod[:, hs], axis=1, keepdims=True) * head_dim ** -0.5 + bias[:, h:h + 1]
        e = jnp.exp(s - jnp.max(s, axis=0, keepdims=True))
        den = jnp.sum(e, axis=0, keepdims=True)
        o_ref[:, hs] = jnp.sum(e * vs[:, hs], axis=0, keepdims=True) / den


def _sample_attend(idx, page_table, q3, knew3, vnew3, rel_pad, cache_k4, cache_v4, layer, n_heads, head_dim):
    DB, topk = idx.shape
    att = n_heads * head_dim
    page = cache_k4.shape[2]
    past = page_table.shape[1] * page
    row = pl.BlockSpec((None, 1, att), lambda b, ix, pt: (b, 0, 0))
    grid_spec = pltpu.PrefetchScalarGridSpec(
        num_scalar_prefetch=2, grid=(DB,),
        in_specs=[row, row, row,
                  pl.BlockSpec((None, topk, 1), lambda b, ix, pt: (b, 0, 0)),
                  pl.BlockSpec(rel_pad.shape, lambda b, ix, pt: (0, 0)),
                  pl.BlockSpec(memory_space=pl.ANY), pl.BlockSpec(memory_space=pl.ANY)],
        out_specs=row,
        scratch_shapes=[pltpu.VMEM((topk, att), F32), pltpu.VMEM((topk, att), F32),
                        pltpu.SemaphoreType.DMA((2,))],
    )
    body = functools.partial(_sample_attend_body, layer=layer, past=past, page=page, topk=topk,
                             n_heads=n_heads, head_dim=head_dim, n_buckets=rel_pad.shape[0])
    return pl.pallas_call(
        body, grid_spec=grid_spec, out_shape=jax.ShapeDtypeStruct((DB, 1, att), F32),
        compiler_params=_params(1), name="sample_attend",
    )(idx, page_table, q3, knew3, vnew3, idx.reshape(DB, topk, 1), rel_pad, cache_k4, cache_v4)


def _softplus(z):
    return jnp.maximum(z, 0.0) + jnp.log1p(jnp.exp(-jnp.abs(z)))


def _one_minus_exp(x):
    e = jnp.exp(x)
    log_e = jnp.log(e)
    return jnp.where(e == 1.0, -x, (1.0 - e) * x / jnp.where(log_e == 0.0, 1.0, log_e))


def _rglru_gates(xc_ref, a_ref, u_ref, wa_ref, wx_ref, ba_ref, bx_ref, lam_ref, n_blocks, bw):
    for n in range(n_blocks):
        cs = slice(n * bw, (n + 1) * bw)
        xb = xc_ref[:, cs]
        xb16 = xb.astype(BF16)
        r = jax.nn.sigmoid(jnp.dot(xb16, wa_ref[n].astype(BF16), preferred_element_type=F32) + ba_ref[:, cs])
        ig = jax.nn.sigmoid(jnp.dot(xb16, wx_ref[n].astype(BF16), preferred_element_type=F32) + bx_ref[:, cs])
        log_a = -RG_C * r * _softplus(-lam_ref[:, cs])
        a_ref[:, cs] = jnp.exp(log_a)
        u_ref[:, cs] = jnp.sqrt(_one_minus_exp(2.0 * log_a)) * ig * xb


def _rglru_seq_body(xr_ref, xg_ref, buf_ref, h0_ref, cw_ref, cb_ref, wa_ref, wx_ref, ba_ref, bx_ref, lam_ref,
                    y_ref, hl_ref, nb_ref, xext_ref, xc_ref, a_ref, u_ref, hc_ref, *, tt, n_blocks, bw):
    i = pl.program_id(1)
    halo = SUBLANES

    @pl.when(i == 0)
    def _():
        xext_ref[0:halo, :] = buf_ref[...]
        hc_ref[...] = h0_ref[...]

    @pl.when(i > 0)
    def _():
        xext_ref[0:halo, :] = xext_ref[tt:tt + halo, :]

    xext_ref[halo:halo + tt, :] = xr_ref[...]
    xc = cb_ref[...] + xext_ref[halo - 3:halo - 3 + tt, :] * cw_ref[0:1, :]
    for j in range(1, CONV_W):
        xc = xc + xext_ref[halo - 3 + j:halo - 3 + j + tt, :] * cw_ref[j:j + 1, :]
    xc_ref[...] = xc
    _rglru_gates(xc_ref, a_ref, u_ref, wa_ref, wx_ref, ba_ref, bx_ref, lam_ref, n_blocks, bw)

    def step(t, h):
        row = pl.ds(t, 1)
        h = a_ref[row, :] * h + u_ref[row, :]
        u_ref[row, :] = h
        return h

    h = lax.fori_loop(0, tt, step, hc_ref[...], unroll=8)
    hc_ref[...] = h
    y_ref[...] = (u_ref[...] * jax.nn.gelu(xg_ref[...])).astype(y_ref.dtype)

    @pl.when(i == pl.num_programs(1) - 1)
    def _():
        hl_ref[...] = h
        nb_ref[...] = xext_ref[tt:tt + halo, :]


def _rglru_seq(xrg, buf8, h0, lw, layer, *, B, T, DR):
    tt = min(T, 256)
    nt = T // tt
    n_blocks, bw = lw["w_rg_a"].shape[1:3]

    def per_layer(shape):
        return pl.BlockSpec((None,) + shape, lambda b, i: (layer,) + (0,) * len(shape))

    per_seq8 = pl.BlockSpec((None, SUBLANES, DR), lambda b, i: (b, 0, 0))
    per_seq1 = pl.BlockSpec((None, 1, DR), lambda b, i: (b, 0, 0))
    body = functools.partial(_rglru_seq_body, tt=tt, n_blocks=n_blocks, bw=bw)
    return pl.pallas_call(
        body, grid=(B, nt),
        in_specs=[pl.BlockSpec((tt, DR), lambda b, i: (b * nt + i, 0)),
                  pl.BlockSpec((tt, DR), lambda b, i: (b * nt + i, 1)),
                  per_seq8, per_seq1,
                  per_layer((CONV_W, DR)), per_layer((1, DR)),
                  per_layer((n_blocks, bw, bw)), per_layer((n_blocks, bw, bw)),
                  per_layer((1, DR)), per_layer((1, DR)), per_layer((1, DR))],
        out_specs=[pl.BlockSpec((tt, DR), lambda b, i: (b * nt + i, 0)), per_seq1, per_seq8],
        out_shape=[jax.ShapeDtypeStruct((B * T, DR), BF16), jax.ShapeDtypeStruct((B, 1, DR), F32),
                   jax.ShapeDtypeStruct((B, SUBLANES, DR), F32)],
        scratch_shapes=[pltpu.VMEM((tt + SUBLANES, DR), F32), pltpu.VMEM((tt, DR), F32),
                        pltpu.VMEM((tt, DR), F32), pltpu.VMEM((tt, DR), F32), pltpu.VMEM((1, DR), F32)],
        compiler_params=_params(2), name="rglru_seq",
    )(xrg, xrg, buf8, h0, lw["conv_w"], lw["conv_b"], lw["w_rg_a"], lw["w_rg_x"],
      lw["b_rg_a"], lw["b_rg_x"], lw["rg_lambda"])


def _rglru_step_body(xr_ref, xg_ref, buf_ref, h0_ref, cw_ref, cb_ref, wa_ref, wx_ref, ba_ref, bx_ref, lam_ref,
                     y_ref, hl_ref, nb_ref, xc_ref, a_ref, u_ref, *, n_blocks, bw):
    xr = xr_ref[...]
    xc = cb_ref[...] + xr * cw_ref[CONV_W - 1:CONV_W, :]
    for j in range(CONV_W - 1):
        xc = xc + buf_ref[j] * cw_ref[j:j + 1, :]
    xc_ref[...] = xc
    _rglru_gates(xc_ref, a_ref, u_ref, wa_ref, wx_ref, ba_ref, bx_ref, lam_ref, n_blocks, bw)
    h = a_ref[...] * h0_ref[...] + u_ref[...]
    hl_ref[...] = h
    y_ref[...] = h * jax.nn.gelu(xg_ref[...])
    for j in range(CONV_W - 2):
        nb_ref[j] = buf_ref[j + 1]
    nb_ref[CONV_W - 2] = xr


def _rglru_step(xrg, bufT, h0, lw, layer, *, DB, DR):
    n_blocks, bw = lw["w_rg_a"].shape[1:3]

    def per_layer(shape):
        return pl.BlockSpec((None,) + shape, lambda i: (layer,) + (0,) * len(shape))

    tile = pl.BlockSpec((DB, DR), lambda i: (0, 0))
    body = functools.partial(_rglru_step_body, n_blocks=n_blocks, bw=bw)
    return pl.pallas_call(
        body, grid=(1,),
        in_specs=[tile, pl.BlockSpec((DB, DR), lambda i: (0, 1)),
                  per_layer((CONV_W - 1, DB, DR)), per_layer((DB, DR)),
                  per_layer((CONV_W, DR)), per_layer((1, DR)),
                  per_layer((n_blocks, bw, bw)), per_layer((n_blocks, bw, bw)),
                  per_layer((1, DR)), per_layer((1, DR)), per_layer((1, DR))],
        out_specs=[tile, tile, pl.BlockSpec((CONV_W - 1, DB, DR), lambda i: (0, 0, 0))],
        out_shape=[jax.ShapeDtypeStruct((DB, DR), F32), jax.ShapeDtypeStruct((DB, DR), F32),
                   jax.ShapeDtypeStruct((CONV_W - 1, DB, DR), F32)],
        scratch_shapes=[pltpu.VMEM((DB, DR), F32)] * 3,
        compiler_params=_params(1), name="rglru_step",
    )(xrg, xrg, bufT, h0, lw["conv_w"], lw["conv_b"], lw["w_rg_a"], lw["w_rg_x"],
      lw["b_rg_a"], lw["b_rg_x"], lw["rg_lambda"])


def _merge_body(ya_ref, yr_ref, wa_ref, wr_ref, ga_ref, gr_ref, o_ref, wab_ref, wrb_ref):
    @pl.when(pl.program_id(1) == 0)
    def _():
        wab_ref[...] = wa_ref[...].astype(BF16)
        wrb_ref[...] = wr_ref[...].astype(BF16)

    za = jnp.dot(ya_ref[...], wab_ref[...], preferred_element_type=F32)
    zr = jnp.dot(yr_ref[...], wrb_ref[...], preferred_element_type=F32)
    o_ref[...] = (jax.nn.sigmoid(ga_ref[...]) * za + jax.nn.sigmoid(gr_ref[...]) * zr).astype(o_ref.dtype)


def _merge(ya, yr, w_up_att, w_up_rnn, gates, layer):
    M, att = ya.shape
    dr = yr.shape[1]
    D = w_up_att.shape[-1]
    tm = min(M, 512)
    tn = _pick_tile(D, 0)
    nj = D // tn
    return pl.pallas_call(
        _merge_body, grid=(nj, M // tm),
        in_specs=[pl.BlockSpec((tm, att), lambda j, i: (i, 0)),
                  pl.BlockSpec((tm, dr), lambda j, i: (i, 0)),
                  pl.BlockSpec((None, att, tn), lambda j, i: (layer, 0, j)),
                  pl.BlockSpec((None, dr, tn), lambda j, i: (layer, 0, j)),
                  pl.BlockSpec((tm, tn), lambda j, i: (i, j)),
                  pl.BlockSpec((tm, tn), lambda j, i: (i, nj + j))],
        out_specs=pl.BlockSpec((tm, tn), lambda j, i: (i, j)),
        out_shape=jax.ShapeDtypeStruct((M, D), BF16),
        scratch_shapes=[pltpu.VMEM((att, tn), BF16), pltpu.VMEM((dr, tn), BF16)],
        compiler_params=_params(2), name="merge",
    )(ya, yr, w_up_att, w_up_rnn, gates, gates)


def _mlp_body(h_ref, wi_ref, wo_ref, o_ref):
    f = pl.program_id(1)
    hid = jnp.dot(h_ref[...], wi_ref[...], preferred_element_type=F32)
    hid = jnp.square(jnp.maximum(hid, 0.0)).astype(BF16)
    part = jnp.dot(hid, wo_ref[...], preferred_element_type=F32)

    @pl.when(f == 0)
    def _():
        o_ref[...] = part

    @pl.when(f > 0)
    def _():
        o_ref[...] += part


def _mlp(h, wi, wo):
    M, D = h.shape
    dff = wi.shape[1]
    tm = min(M, 512)
    tf = _pick_tile(dff, 0)
    return pl.pallas_call(
        _mlp_body, grid=(M // tm, dff // tf),
        in_specs=[pl.BlockSpec((tm, D), lambda i, f: (i, 0)),
                  pl.BlockSpec((D, tf), lambda i, f: (0, f)),
                  pl.BlockSpec((tf, D), lambda i, f: (f, 0))],
        out_specs=pl.BlockSpec((tm, D), lambda i, f: (i, 0)),
        out_shape=jax.ShapeDtypeStruct((M, D), F32),
        compiler_params=_params(2), name="mlp",
    )(h, wi, wo)


def _silu_bf16(c):
    return (c * jax.nn.sigmoid(c)).astype(BF16)


def _in_projection(h, w_in, wseg, layer, dims):
    att, idw, dr, d = dims["att"], dims["idx_width"], dims["dr"], dims["d"]
    q_scale = dims["head_dim"] ** -0.5
    iq_scale = dims["idx_dim"] ** -0.5
    idx_dim = dims["idx_dim"]
    (q,) = _matmul(h, w_in, layer, 0, att, [(att, BF16)], lambda acc: [acc * q_scale], name="proj_q")
    kv, kvb = _matmul(h, w_in, layer, att, 2 * att, [(2 * att, F32), (2 * att, BF16)],
                      lambda acc: [acc, acc], name="proj_kv")
    (iq,) = _matmul(h, w_in, layer, 3 * att, idw, [(idw, BF16)], lambda acc: [acc * iq_scale], name="proj_iq")
    ikw, ikb = _matmul(h, wseg["ikw"], 0, 0, 2 * LANES, [(2 * LANES, F32), (idx_dim, BF16)],
                       lambda acc: [acc, acc[:, :idx_dim]], name="proj_ikw")
    (xrg,) = _matmul(h, wseg["rg"], 0, 0, 2 * dr, [(2 * dr, F32)], lambda acc: [acc], name="proj_rg")
    (gates,) = _matmul(h, wseg["gate"], 0, 0, 2 * d, [(2 * d, F32)], lambda acc: [acc], name="proj_gate")
    return q, kv, kvb, iq, ikw, ikb, xrg, gates


def _finish_layer(x, ya, yr, gates, st, st_next, lw, wseg, norm_g3, layer):
    d = x.shape[1]
    merged = _merge(ya, yr, lw["w_up_att"], lw["w_up_rnn"], gates, layer)
    (y,) = _matmul(merged, lw["w_o"], layer, 0, d, [(d, F32)], lambda acc: [acc], name="proj_o")
    x, h2 = _postnorm(x, y, norm_g3, 4 * layer + 1, st, 2, nxt=(4 * layer + 2, st, 3, 4))
    ff = _mlp(h2, wseg["mlp_in"], wseg["mlp_out"])
    nxt = None if st_next is None else (4 * (layer + 1), st_next, 0, 1)
    return _postnorm(x, ff, norm_g3, 4 * layer + 3, st, 5, nxt=nxt)


def kernel(x_prompt, x_sample, cache_k, cache_v, cache_idx_k, state_rglru_h, state_conv, page_table, c_prompt, c_sample, w_ada, b_ada, norm_g, w_in, rel_bias, conv_w, conv_b, w_rg_a, b_rg_a, w_rg_x, b_rg_x, rg_lambda, w_up_att, w_up_rnn, w_o, w_mlp_in, w_mlp_out):
    B, T, D = x_prompt.shape
    DB, t_new, _ = x_sample.shape
    assert t_new == 1, "the sample group decodes one token per sequence"
    L = w_ada.shape[0]
    n_pool, page, n_heads, head_dim = cache_k.shape[1:]
    att = n_heads * head_dim
    idx_dim = cache_idx_k.shape[-1]
    DR = state_rglru_h.shape[-1]
    in_w = w_in.shape[-1]
    idx_heads = (in_w - 3 * att - idx_dim - 2 * DR - 2 * D) // (idx_dim + 1)
    idw = idx_heads * idx_dim
    n_pages = page_table.shape[1]
    past = n_pages * page
    assert idx_dim == LANES and idx_heads <= LANES and n_heads <= LANES and page == LANES
    dims = dict(att=att, idx_width=idw, dr=DR, d=D, head_dim=head_dim, idx_dim=idx_dim)
    o_ik = 3 * att + idw
    o_xr = o_ik + idx_dim + idx_heads
    o_ga = o_xr + 2 * DR

    tq = min(T, 256)
    nq = T // tq
    topk_p = min(TOPK_MAX, T // 4)
    topk_s = min(TOPK_MAX, (past + 1) // 4)
    Mp = B * T
    Ms = 2 * SUBLANES
    assert DB <= Ms and B + Ms <= 4 * SUBLANES and tq % LANES == 0 and T % tq == 0

    c_all = jnp.zeros((4 * SUBLANES, D), F32).at[:B].set(c_prompt).at[B:B + DB].set(c_sample)
    b_ada3 = b_ada.reshape(L, 1, 6 * D)
    norm_g3 = norm_g.reshape(L * 4, 1, D)
    lw_all = dict(conv_w=conv_w, conv_b=conv_b.reshape(L, 1, DR), w_rg_a=w_rg_a, w_rg_x=w_rg_x,
                  b_rg_a=b_rg_a.reshape(L, 1, DR), b_rg_x=b_rg_x.reshape(L, 1, DR),
                  rg_lambda=rg_lambda.reshape(L, 1, DR), w_up_att=w_up_att, w_up_rnn=w_up_rnn, w_o=w_o)

    bias_tiles = _bias_tiles(rel_bias, tq)
    rel_pad = jnp.pad(rel_bias, ((0, 0), (0, LANES - n_heads)))
    cache_k4 = cache_k.reshape(L, n_pool, page, att)
    cache_v4 = cache_v.reshape(L, n_pool, page, att)
    conv_sT = jnp.swapaxes(state_conv, 1, 2)
    zero_buf8 = jnp.zeros((B, SUBLANES, DR), F32)
    zero_h = jnp.zeros((B, 1, DR), F32)

    xp = x_prompt.reshape(Mp, D)
    xs = jnp.pad(x_sample.reshape(DB, D), ((0, Ms - DB), (0, 0)))
    hp = hs = None
    outs_p = [[] for _ in range(5)]
    outs_s = [[] for _ in range(5)]

    tn_ada = _pick_tile(6 * D, 0)
    streams_p, streams_s = [], []
    for l in range(L):
        (mod,) = _matmul(c_all, w_ada, l, 0, 6 * D, [(6 * D, F32)], lambda acc, b: [acc + b],
                         extras=[(b_ada3, pl.BlockSpec((None, 1, tn_ada), lambda j, i, l=l: (l, 0, j)))],
                         a_fn=_silu_bf16, name="ada")
        streams_p.append(_Stream(mod[:B].reshape(B, 1, 6 * D), T, Mp, D))
        streams_s.append(_Stream(mod[B:B + Ms], 1, Ms, D))
    streams_p.append(None)
    streams_s.append(None)

    for l in range(L):
        st_p, st_s = streams_p[l], streams_s[l]
        wseg = dict(
            ikw=jnp.pad(w_in[l, :, o_ik:o_xr], ((0, 0), (0, 2 * LANES - idx_dim - idx_heads))).astype(BF16)[None],
            rg=w_in[l, :, o_xr:o_ga].astype(BF16)[None],
            gate=w_in[l, :, o_ga:].astype(BF16)[None],
            mlp_in=w_mlp_in[l].astype(BF16), mlp_out=w_mlp_out[l].astype(BF16))
        if l == 0:
            hp = _prenorm(xp, norm_g3, 0, st_p, 0, 1)
            hs = _prenorm(xs, norm_g3, 0, st_s, 0, 1)

        q, kv, kvb, iq, ikw, ikb, xrg, gates = _in_projection(hp, w_in, wseg, l, dims)
        qT = jnp.swapaxes(q.reshape(B * nq, tq, att), 1, 2)
        iqT = jnp.swapaxes(iq.reshape(B * nq, tq, idw), 1, 2)
        vT = jnp.swapaxes(kvb[:, att:].reshape(B * nq, tq, att), 1, 2)
        wT = jnp.swapaxes(ikw[:, idx_dim:].reshape(B * nq, tq, LANES), 1, 2)
        yT = _dsa_prompt(qT, kvb, vT, iqT, ikb, wT, bias_tiles, B=B, T=T, tq=tq, topk=topk_p,
                         n_heads=n_heads, head_dim=head_dim, idx_heads=idx_heads, idx_dim=idx_dim)
        ya = jnp.swapaxes(yT, 1, 2).reshape(Mp, att)
        yr, h_last, nbuf = _rglru_seq(xrg, zero_buf8, zero_h, lw_all, l, B=B, T=T, DR=DR)
        xp, hp = _finish_layer(xp, ya, yr, gates, st_p, streams_p[l + 1], lw_all, wseg, norm_g3, l)
        outs_p[0].append(kv[:, :att].reshape(B, T, n_heads, head_dim))
        outs_p[1].append(kv[:, att:].reshape(B, T, n_heads, head_dim))
        outs_p[2].append(ikw[:, :idx_dim].reshape(B, T, idx_dim))
        outs_p[3].append(h_last.reshape(B, DR))
        outs_p[4].append(nbuf[:, SUBLANES - (CONV_W - 1):])

        q, kv, kvb, iq, ikw, ikb, xrg, gates = _in_projection(hs, w_in, wseg, l, dims)
        k_new = kv[:DB, :att]
        v_new = kv[:DB, att:]
        ik_new = ikw[:DB, :idx_dim]
        iq3 = iq[:DB].reshape(DB, idx_heads, idx_dim)
        w3 = jnp.broadcast_to(ikw[:DB, idx_dim:idx_dim + idx_heads, None], (DB, idx_heads, page))
        new_pages = jnp.zeros((DB, page, idx_dim), F32).at[:, 0].set(ik_new)
        scores = _sample_scores(page_table, iq3, w3, cache_idx_k, new_pages, l)
        idx = _sample_topk(scores.reshape(DB, -1), topk_s)
        q_f32 = q[:DB].astype(F32) * head_dim ** 0.5
        ya_s = _sample_attend(idx, page_table, q_f32.reshape(DB, 1, att), k_new.reshape(DB, 1, att),
                              v_new.reshape(DB, 1, att), rel_pad, cache_k4, cache_v4, l, n_heads, head_dim)
        ya_s = jnp.pad(ya_s.reshape(DB, att), ((0, Ms - DB), (0, 0))).astype(BF16)
        y_s, h_new, nbufT = _rglru_step(xrg, conv_sT, state_rglru_h, lw_all, l, DB=DB, DR=DR)
        yr_s = jnp.pad(y_s, ((0, Ms - DB), (0, 0))).astype(BF16)
        xs, hs = _finish_layer(xs, ya_s, yr_s, gates, st_s, streams_s[l + 1], lw_all, wseg, norm_g3, l)
        outs_s[0].append(k_new.reshape(DB, 1, n_heads, head_dim))
        outs_s[1].append(v_new.reshape(DB, 1, n_heads, head_dim))
        outs_s[2].append(ik_new.reshape(DB, 1, idx_dim))
        outs_s[3].append(h_new)
        outs_s[4].append(jnp.swapaxes(nbufT, 0, 1))

    y_prompt = xp.reshape(B, T, D)
    y_sample = xs[:DB].reshape(DB, 1, D)
    return (y_prompt, y_sample, *[jnp.stack(o) for o in outs_p], *[jnp.stack(o) for o in outs_s])
```

```python
import functools
import math

import jax
import jax.numpy as jnp
from jax import lax
from jax.experimental import pallas as pl
from jax.experimental.pallas import tpu as pltpu

F32 = jnp.float32
BF16 = jnp.bfloat16
I32 = jnp.int32

V7X_VMEM_BYTES = 64 * 1024 * 1024
VMEM_LIMIT = V7X_VMEM_BYTES - 8 * 1024 * 1024
LANES = 128
SUBLANES = 8

TOPK_MAX = 256
MAX_DISTANCE = 128
RG_C = 8.0
NORM_EPS = 1e-6
CONV_W = 4
MASK_NEG = -1e30
LOG2_E = math.log2(math.e)
ATTN_HEAD_GROUP = 4
INT_MIN = -2 ** 31
INT_MAX = 2 ** 31 - 1


def _params(n_axes):
    return pltpu.CompilerParams(dimension_semantics=("arbitrary",) * n_axes, vmem_limit_bytes=VMEM_LIMIT)


def _pick_tile(n, col0, candidates=(512, 256, 128)):
    for t in candidates:
        if n % t == 0 and col0 % t == 0:
            return t
    raise ValueError(f"no lane tile for width {n} at column {col0}")


def _mm_body(*refs, n_extra, n_out, cast_w, a_fn, epilogue):
    a_ref, w_ref = refs[0], refs[1]
    extra = refs[2:2 + n_extra]
    outs = refs[2 + n_extra:2 + n_extra + n_out]
    if cast_w:
        wb_ref = refs[2 + n_extra + n_out]

        @pl.when(pl.program_id(1) == 0)
        def _():
            wb_ref[...] = w_ref[...].astype(BF16)

        w = wb_ref[...]
    else:
        w = w_ref[...]
    a = a_ref[...]
    if a_fn is not None:
        a = a_fn(a)
    acc = jnp.dot(a, w, preferred_element_type=F32)
    res = epilogue(acc, *[e[...] for e in extra])
    for o_ref, r in zip(outs, res):
        o_ref[...] = r.astype(o_ref.dtype)


def _matmul(a, w, layer, col0, n, outs, epilogue, *, extras=(), a_fn=None, tn=None, name="mm"):
    M, K = a.shape
    tm = min(M, 1024)
    tn = _pick_tile(n, col0) if tn is None else tn
    assert M % tm == 0 and n % tn == 0 and col0 % tn == 0
    cast_w = w.dtype != BF16
    cb = col0 // tn
    in_specs = [pl.BlockSpec((tm, K), lambda j, i: (i, 0)),
                pl.BlockSpec((None, K, tn), lambda j, i: (layer, 0, cb + j))]
    in_specs += [spec for _, spec in extras]
    out_specs = [pl.BlockSpec((tm, tn * wd // n), lambda j, i: (i, j)) for wd, _ in outs]
    out_shape = [jax.ShapeDtypeStruct((M, wd), dt) for wd, dt in outs]
    body = functools.partial(_mm_body, n_extra=len(extras), n_out=len(outs), cast_w=cast_w,
                             a_fn=a_fn, epilogue=epilogue)
    return pl.pallas_call(
        body, grid=(n // tn, M // tm), in_specs=in_specs, out_specs=out_specs, out_shape=out_shape,
        scratch_shapes=[pltpu.VMEM((K, tn), BF16)] if cast_w else [],
        compiler_params=_params(2), name=name,
    )(a, w, *[arr for arr, _ in extras])


def _repack_body(a_ref, b_ref, o_ref, *, shift):
    o_ref[...] = jnp.concatenate([a_ref[:, shift:], b_ref[:, :shift]], axis=1).astype(o_ref.dtype)


def _repack_columns(w, layer, col0, n):
    K, total = w.shape[1:]
    shift = col0 % LANES
    first = col0 // LANES
    assert shift and n % LANES == 0 and (first + n // LANES) * LANES < total + LANES
    return pl.pallas_call(
        functools.partial(_repack_body, shift=shift), grid=(n // LANES,),
        in_specs=[pl.BlockSpec((None, K, LANES), lambda j: (layer, 0, first + j)),
                  pl.BlockSpec((None, K, LANES), lambda j: (layer, 0, first + j + 1))],
        out_specs=pl.BlockSpec((K, LANES), lambda j: (0, j)),
        out_shape=jax.ShapeDtypeStruct((K, n), BF16),
        compiler_params=_params(1), name="repack",
    )(w, w)


def _rms(x, g):
    return x * lax.rsqrt(jnp.mean(x * x, axis=-1, keepdims=True) + NORM_EPS) * g


def _prenorm_body(x_ref, g_ref, sh_ref, sc_ref, h_ref):
    h = _rms(x_ref[...], g_ref[...]) * (1.0 + sc_ref[...]) + sh_ref[...]
    h_ref[...] = h.astype(h_ref.dtype)


def _postnorm_body(*refs, with_next):
    if with_next:
        x_ref, y_ref, gpost_ref, gate_ref, gpre_ref, sh_ref, sc_ref, xo_ref, h_ref = refs
    else:
        x_ref, y_ref, gpost_ref, gate_ref, xo_ref = refs
    x = x_ref[...] + gate_ref[...] * _rms(y_ref[...], gpost_ref[...])
    xo_ref[...] = x
    if with_next:
        h = _rms(x, gpre_ref[...]) * (1.0 + sc_ref[...]) + sh_ref[...]
        h_ref[...] = h.astype(h_ref.dtype)


class _Stream:
    def __init__(self, mod, rows_per_seq, n_rows, d):
        self.mod = mod
        self.rows_per_seq = rows_per_seq
        self.tm = min(256, n_rows)
        self.d = d

    def mod_spec(self, chunk):
        tm, d = self.tm, self.d
        if self.rows_per_seq > 1:
            rps = self.rows_per_seq
            return pl.BlockSpec((None, 1, d), lambda i: (i * tm // rps, 0, chunk))
        return pl.BlockSpec((tm, d), lambda i: (i, chunk))


def _norm_spec(g_index, d):
    return pl.BlockSpec((None, 1, d), lambda i: (g_index, 0, 0))


def _prenorm(x, norm_g3, g_index, st, shift_chunk, scale_chunk):
    M, D = x.shape
    tm = st.tm
    row = pl.BlockSpec((tm, D), lambda i: (i, 0))
    return pl.pallas_call(
        _prenorm_body, grid=(M // tm,),
        in_specs=[row, _norm_spec(g_index, D), st.mod_spec(shift_chunk), st.mod_spec(scale_chunk)],
        out_specs=row, out_shape=jax.ShapeDtypeStruct((M, D), BF16),
        compiler_params=_params(1), name="prenorm",
    )(x, norm_g3, st.mod, st.mod)


def _postnorm(x, y, norm_g3, g_post, st, gate_chunk, nxt=None):
    M, D = x.shape
    tm = st.tm
    row = pl.BlockSpec((tm, D), lambda i: (i, 0))
    in_specs = [row, row, _norm_spec(g_post, D), st.mod_spec(gate_chunk)]
    args = [x, y, norm_g3, st.mod]
    out_specs = [row]
    out_shape = [jax.ShapeDtypeStruct((M, D), F32)]
    if nxt is not None:
        g_pre, st_n, sh_c, sc_c = nxt
        in_specs += [_norm_spec(g_pre, D), st_n.mod_spec(sh_c), st_n.mod_spec(sc_c)]
        args += [norm_g3, st_n.mod, st_n.mod]
        out_specs.append(row)
        out_shape.append(jax.ShapeDtypeStruct((M, D), BF16))
    res = pl.pallas_call(
        functools.partial(_postnorm_body, with_next=nxt is not None), grid=(M // tm,),
        in_specs=in_specs, out_specs=out_specs, out_shape=out_shape,
        compiler_params=_params(1), name="postnorm",
    )(*args)
    return res if nxt is not None else (res[0], None)


def _t5_bucket(dist, n_buckets):
    max_exact = n_buckets // 2
    d = jnp.maximum(dist, 0)
    d_f = jnp.maximum(d, 1).astype(F32)
    large = max_exact + (jnp.log(d_f / max_exact) / math.log(MAX_DISTANCE / max_exact)
                         * (n_buckets - max_exact)).astype(I32)
    large = jnp.minimum(large, n_buckets - 1)
    return jnp.where(d < max_exact, d, large)


def _bias_tiles_body(rel_ref, o_ref, *, tq, n_buckets):
    which = pl.program_id(0)
    h = pl.program_id(1)
    r = lax.broadcasted_iota(I32, (tq, tq), 0)
    c = lax.broadcasted_iota(I32, (tq, tq), 1)
    bucket = _t5_bucket(which * tq + c - r, n_buckets)
    far = rel_ref[n_buckets - 1, h]
    acc = jnp.zeros((tq, tq), F32)
    for n in range(n_buckets - 1):
        acc = jnp.where(bucket == n, rel_ref[n, h] - far, acc)
    o_ref[...] = acc * LOG2_E


def _bias_tiles(rel_bias, tq):
    nb, n_heads = rel_bias.shape
    return pl.pallas_call(
        functools.partial(_bias_tiles_body, tq=tq, n_buckets=nb), grid=(2, n_heads),
        in_specs=[pl.BlockSpec(memory_space=pltpu.SMEM)],
        out_specs=pl.BlockSpec((None, None, tq, tq), lambda w, h: (w, h, 0, 0)),
        out_shape=jax.ShapeDtypeStruct((2, n_heads, tq, tq), F32),
        compiler_params=_params(2), name="bias_tiles",
    )(rel_bias)


def _attn_body(qT_ref, k_ref, vT_ref, iqT_ref, ik_ref, wT_ref, bias_ref, o_ref,
               key_ref, mask_ref, p_ref, m_ref, l_ref, acc_ref, lg_ref,
               *, topk, n_heads, head_dim, idx_heads, idx_dim, tq, n_chunks):
    i = pl.program_id(1)
    nck = i + 1
    w_scale = idx_heads ** -0.5
    q_pos = i * tq + lax.broadcasted_iota(I32, (tq, tq), 1)
    r_loc = lax.broadcasted_iota(I32, (tq, tq), 0)

    def rows(c):
        return pl.ds(pl.multiple_of(c * tq, tq), tq)

    def score_chunk(c, carry):
        ik_c = ik_ref[rows(c), :]
        sc = None
        for h in range(idx_heads):
            s = jnp.dot(ik_c, iqT_ref[h * idx_dim:(h + 1) * idx_dim, :], preferred_element_type=F32)
            contrib = jnp.maximum(s, 0.0) * (wT_ref[h:h + 1, :] * w_scale)
            sc = contrib if sc is None else sc + contrib
        bits = pltpu.bitcast(sc + 0.0, I32)
        key = jnp.where(bits < 0, bits ^ INT_MAX, bits)
        key_ref[rows(c), :] = jnp.where(c * tq + r_loc <= q_pos, key, INT_MIN)
        return carry

    lax.fori_loop(0, nck, score_chunk, 0)

    def count(pred):
        def body(c, cnt):
            hit = pred(key_ref[rows(c), :], c).astype(I32).reshape(tq // SUBLANES, SUBLANES, tq)
            return cnt + jnp.sum(hit, axis=0)
        cnt = lax.fori_loop(0, nck, body, jnp.zeros((SUBLANES, tq), I32))
        return jnp.sum(cnt, axis=0, keepdims=True)

    thr = jnp.where(count(lambda k, c: k >= 0) >= topk, 0, INT_MIN).astype(I32)

    def thr_bit(s, thr):
        cand = thr | jnp.left_shift(jnp.int32(1), 30 - s)
        return jnp.where(count(lambda k, c: k >= cand) >= topk, cand, thr)

    thr = lax.fori_loop(0, 31, thr_bit, thr)

    n_gt = count(lambda k, c: k > thr)
    n_ge = count(lambda k, c: k >= thr)
    need = topk - n_gt
    p_ref[...] = jnp.full((1, tq), INT_MAX, I32)
    has_ties = jnp.max(jnp.where((n_ge > topk) & (thr > INT_MIN), 1, 0)) > 0

    @pl.when(has_ties)
    def _():
        n_bits = max(1, (n_chunks * tq - 1).bit_length())

        def pos_bit(s, p):
            cand = p | jnp.left_shift(jnp.int32(1), n_bits - 1 - s)
            below = count(lambda k, c: (k == thr) & (c * tq + r_loc < cand))
            return jnp.where(below < need, cand, p)

        p_ref[...] = lax.fori_loop(0, n_bits, pos_bit, jnp.zeros((1, tq), I32))

    p_last = p_ref[...]

    def mask_chunk(c, carry):
        k = key_ref[rows(c), :]
        pos = c * tq + r_loc
        keep = ((k > thr) | ((k == thr) & (pos <= p_last))) & (pos <= q_pos)
        mask_ref[c] = jnp.where(keep, 0.0, MASK_NEG)
        return carry

    lax.fori_loop(0, nck, mask_chunk, 0)

    m_ref[...] = jnp.full(m_ref.shape, MASK_NEG, F32)
    l_ref[...] = jnp.zeros(l_ref.shape, F32)
    acc_ref[...] = jnp.zeros(acc_ref.shape, F32)
    group = lg_ref.shape[0]

    def logits(c, carry, h0, near):
        mask_c = mask_ref[c]
        for g in range(group):
            h = h0 + g
            hs = slice(h * head_dim, (h + 1) * head_dim)
            s = jnp.dot(k_ref[rows(c), hs], qT_ref[hs, :], preferred_element_type=F32) + mask_c
            if near:
                s = s + bias_ref[i - c, h]
            lg_ref[g, rows(c), :] = s
            m_ref[h] = jnp.maximum(m_ref[h], jnp.max(s, axis=0, keepdims=True))
        return carry

    def values(c, carry, h0):
        for g in range(group):
            h = h0 + g
            hs = slice(h * head_dim, (h + 1) * head_dim)
            p = jnp.exp2(lg_ref[g, rows(c), :] - m_ref[h])
            l_ref[h] += jnp.sum(p, axis=0, keepdims=True)
            acc_ref[h] += jnp.dot(vT_ref[c, hs, :], p.astype(BF16), preferred_element_type=F32)
        return carry

    c_near = jnp.maximum(i - 1, 0)
    for h0 in range(0, n_heads, group):
        lax.fori_loop(0, c_near, functools.partial(logits, h0=h0, near=False), 0)
        lax.fori_loop(c_near, nck, functools.partial(logits, h0=h0, near=True), 0)
        lax.fori_loop(0, nck, functools.partial(values, h0=h0), 0)
    for h in range(n_heads):
        hs = slice(h * head_dim, (h + 1) * head_dim)
        o_ref[hs, :] = (acc_ref[h] / l_ref[h]).astype(o_ref.dtype)


def _dsa_prompt(qT, kb, vT, iqT, ikb, wT, bias, *, B, T, tq, topk, n_heads, head_dim, idx_heads, idx_dim):
    nq = T // tq
    att = n_heads * head_dim
    once = dict(pipeline_mode=pl.Buffered(1))
    body = functools.partial(_attn_body, topk=topk, n_heads=n_heads, head_dim=head_dim,
                             idx_heads=idx_heads, idx_dim=idx_dim, tq=tq, n_chunks=nq)
    return pl.pallas_call(
        body, grid=(B, nq),
        in_specs=[
            pl.BlockSpec((None, att, tq), lambda b, i: (b * nq + i, 0, 0)),
            pl.BlockSpec((T, att), lambda b, i: (b, 0), **once),
            pl.BlockSpec((nq, att, tq), lambda b, i: (b, 0, 0), **once),
            pl.BlockSpec((None, idx_heads * idx_dim, tq), lambda b, i: (b * nq + i, 0, 0)),
            pl.BlockSpec((T, idx_dim), lambda b, i: (b, 0), **once),
            pl.BlockSpec((None, wT.shape[1], tq), lambda b, i: (b * nq + i, 0, 0)),
            pl.BlockSpec((2, n_heads, tq, tq), lambda b, i: (0, 0, 0, 0), **once),
        ],
        out_specs=pl.BlockSpec((None, att, tq), lambda b, i: (b * nq + i, 0, 0)),
        out_shape=jax.ShapeDtypeStruct((B * nq, att, tq), BF16),
        scratch_shapes=[pltpu.VMEM((T, tq), I32), pltpu.VMEM((nq, tq, tq), F32), pltpu.VMEM((1, tq), I32),
                        pltpu.VMEM((n_heads, 1, tq), F32), pltpu.VMEM((n_heads, 1, tq), F32),
                        pltpu.VMEM((n_heads, head_dim, tq), F32),
                        pltpu.VMEM((math.gcd(ATTN_HEAD_GROUP, n_heads), T, tq), F32)],
        compiler_params=_params(2), name="dsa_prompt",
    )(qT, kb, vT, iqT, ikb, wT, bias)


def _sample_scores_body(pt_ref, iq_ref, w_ref, *refs, n_groups, idx_heads):
    page_refs, new_ref, o_ref = refs[:SUBLANES], refs[SUBLANES], refs[SUBLANES + 1]
    j = pl.program_id(1)

    def page_scores(keys):
        s = lax.dot_general(iq_ref[...], keys.astype(BF16), (((1,), (1,)), ((), ())),
                            preferred_element_type=F32)
        w = w_ref[...] * idx_heads ** -0.5
        return jnp.sum(jnp.maximum(s, 0.0) * w, axis=0, keepdims=True) + 0.0

    group = pl.ds(pl.multiple_of(j * SUBLANES, SUBLANES), SUBLANES)

    @pl.when(j < n_groups)
    def _():
        o_ref[group, :] = jnp.concatenate([page_scores(r[...]) for r in page_refs], axis=0)

    @pl.when(j == n_groups)
    def _():
        sc = jnp.broadcast_to(page_scores(new_ref[...]), (SUBLANES, new_ref.shape[0]))
        lane = lax.broadcasted_iota(I32, sc.shape, 1)
        row = lax.broadcasted_iota(I32, sc.shape, 0)
        o_ref[group, :] = jnp.where((lane == 0) & (row == 0), sc, -jnp.inf)


def _sample_scores(page_table, iq3, w3, cache_idx_k, new_pages, layer):
    DB, n_pages = page_table.shape
    _, idx_heads, idx_dim = iq3.shape
    page = cache_idx_k.shape[2]
    assert n_pages % SUBLANES == 0
    n_groups = n_pages // SUBLANES
    rows = n_pages + SUBLANES

    def page_spec(r):
        return pl.BlockSpec(
            (None, None, page, idx_dim),
            lambda b, j, pt: (layer, pt[b, jnp.minimum(j, n_groups - 1) * SUBLANES + r], 0, 0))

    grid_spec = pltpu.PrefetchScalarGridSpec(
        num_scalar_prefetch=1, grid=(DB, n_groups + 1),
        in_specs=[pl.BlockSpec((None, idx_heads, idx_dim), lambda b, j, pt: (b, 0, 0)),
                  pl.BlockSpec((None, idx_heads, page), lambda b, j, pt: (b, 0, 0))]
        + [page_spec(r) for r in range(SUBLANES)]
        + [pl.BlockSpec((None, page, idx_dim), lambda b, j, pt: (b, 0, 0))],
        out_specs=pl.BlockSpec((None, rows, page), lambda b, j, pt: (b, 0, 0)),
    )
    return pl.pallas_call(
        functools.partial(_sample_scores_body, n_groups=n_groups, idx_heads=idx_heads),
        grid_spec=grid_spec, out_shape=jax.ShapeDtypeStruct((DB, rows, page), F32),
        compiler_params=_params(2), name="sample_scores",
    )(page_table, iq3, w3, *([cache_idx_k] * SUBLANES), new_pages)


def _topk_body(s_ref, o_ref, work_ref, *, topk):
    work_ref[...] = s_ref[...]
    lane = lax.broadcasted_iota(I32, s_ref.shape, 1)
    slot = lax.broadcasted_iota(I32, o_ref.shape, 1)

    def body(k, acc):
        s = work_ref[...]
        m = jnp.max(s, axis=1, keepdims=True)
        idx = jnp.min(jnp.where(s == m, lane, INT_MAX), axis=1, keepdims=True)
        work_ref[...] = jnp.where(lane == idx, -jnp.inf, s)
        return jnp.where(slot == k, idx, acc)

    o_ref[...] = lax.fori_loop(0, topk, body, jnp.zeros(o_ref.shape, I32))


def _sample_topk(scores2d, topk):
    DB, n = scores2d.shape
    return pl.pallas_call(
        functools.partial(_topk_body, topk=topk),
        out_shape=jax.ShapeDtypeStruct((DB, topk), I32),
        scratch_shapes=[pltpu.VMEM((DB, n), F32)],
        compiler_params=pltpu.CompilerParams(vmem_limit_bytes=VMEM_LIMIT), name="sample_topk",
    )(scores2d)


def _sample_attend_body(idx_sm, pt_sm, q_ref, knew_ref, vnew_ref, idxc_ref, rel_ref, ck_hbm, cv_hbm, o_ref,
                        kbuf, vbuf, sem, *, layer, past, page, topk, n_heads, head_dim, n_buckets):
    b = pl.program_id(0)

    def row_copies(k):
        p = jnp.minimum(idx_sm[b, k], past - 1)
        phys = pt_sm[b, p // page]
        off = p % page
        return (pltpu.make_async_copy(ck_hbm.at[layer, phys, off], kbuf.at[:, k, :], sem.at[0]),
                pltpu.make_async_copy(cv_hbm.at[layer, phys, off], vbuf.at[:, k, :], sem.at[1]))

    def start(k, carry):
        for cp in row_copies(k):
            cp.start()
        return carry

    def wait(k, carry):
        for cp in row_copies(k):
            cp.wait()
        return carry

    lax.fori_loop(0, topk, start, 0)
    lax.fori_loop(0, topk, wait, 0)

    idxc = idxc_ref[...]
    is_new = idxc >= past
    bucket = _t5_bucket(past - idxc, n_buckets)
    bias = jnp.zeros((topk, LANES), F32)
    for n in range(n_buckets):
        bias = jnp.where(bucket == n, rel_ref[n:n + 1, :], bias)
    for h in range(n_heads):
        row_h = slice(h, h + 1)
        ks = jnp.where(is_new, knew_ref[row_h, :], kbuf[h])
        vs = jnp.where(is_new, vnew_ref[row_h, :], vbuf[h])
        s = jnp.sum(ks * q_ref[row_h, :], axis=1, keepdims=True) * head_dim ** -0.5 + bias[:, h:h + 1]
        e = jnp.exp(s - jnp.max(s, axis=0, keepdims=True))
        den = jnp.sum(e, axis=0, keepdims=True)
        o_ref[row_h, :] = jnp.sum(e * vs, axis=0, keepdims=True) / den


def _sample_attend(idx, page_table, q3, knew3, vnew3, rel_pad, cache_k, cache_v, layer):
    DB, topk = idx.shape
    page, n_heads, head_dim = cache_k.shape[2:]
    past = page_table.shape[1] * page
    row = pl.BlockSpec((None, n_heads, head_dim), lambda b, ix, pt: (b, 0, 0))
    grid_spec = pltpu.PrefetchScalarGridSpec(
        num_scalar_prefetch=2, grid=(DB,),
        in_specs=[row, row, row,
                  pl.BlockSpec((None, topk, 1), lambda b, ix, pt: (b, 0, 0)),
                  pl.BlockSpec(rel_pad.shape, lambda b, ix, pt: (0, 0)),
                  pl.BlockSpec(memory_space=pl.ANY), pl.BlockSpec(memory_space=pl.ANY)],
        out_specs=row,
        scratch_shapes=[pltpu.VMEM((n_heads, topk, head_dim), F32), pltpu.VMEM((n_heads, topk, head_dim), F32),
                        pltpu.SemaphoreType.DMA((2,))],
    )
    body = functools.partial(_sample_attend_body, layer=layer, past=past, page=page, topk=topk,
                             n_heads=n_heads, head_dim=head_dim, n_buckets=rel_pad.shape[0])
    return pl.pallas_call(
        body, grid_spec=grid_spec, out_shape=jax.ShapeDtypeStruct((DB, n_heads, head_dim), F32),
        compiler_params=_params(1), name="sample_attend",
    )(idx, page_table, q3, knew3, vnew3, idx.reshape(DB, topk, 1), rel_pad, cache_k, cache_v)


def _softplus(z):
    return jnp.maximum(z, 0.0) + jnp.log1p(jnp.exp(-jnp.abs(z)))


def _one_minus_exp(x):
    e = jnp.exp(x)
    log_e = jnp.log(e)
    return jnp.where(e == 1.0, -x, (1.0 - e) * x / jnp.where(log_e == 0.0, 1.0, log_e))


def _rglru_gates(xc_ref, a_ref, u_ref, wa_ref, wx_ref, ba_ref, bx_ref, lam_ref, n_blocks, bw):
    for n in range(n_blocks):
        cs = slice(n * bw, (n + 1) * bw)
        xb = xc_ref[:, cs]
        xb16 = xb.astype(BF16)
        r = jax.nn.sigmoid(jnp.dot(xb16, wa_ref[n].astype(BF16), preferred_element_type=F32) + ba_ref[:, cs])
        ig = jax.nn.sigmoid(jnp.dot(xb16, wx_ref[n].astype(BF16), preferred_element_type=F32) + bx_ref[:, cs])
        log_a = -RG_C * r * _softplus(-lam_ref[:, cs])
        a_ref[:, cs] = jnp.exp(log_a)
        u_ref[:, cs] = jnp.sqrt(_one_minus_exp(2.0 * log_a)) * ig * xb


def _rglru_seq_body(xr_ref, xg_ref, buf_ref, h0_ref, cw_ref, cb_ref, wa_ref, wx_ref, ba_ref, bx_ref, lam_ref,
                    y_ref, hl_ref, nb_ref, xext_ref, xc_ref, a_ref, u_ref, hc_ref, *, tt, n_blocks, bw):
    i = pl.program_id(1)
    halo = SUBLANES

    @pl.when(i == 0)
    def _():
        xext_ref[0:halo, :] = buf_ref[...]
        hc_ref[...] = h0_ref[...]

    @pl.when(i > 0)
    def _():
        xext_ref[0:halo, :] = xext_ref[tt:tt + halo, :]

    xext_ref[halo:halo + tt, :] = xr_ref[...]
    xc = cb_ref[...] + xext_ref[halo - 3:halo - 3 + tt, :] * cw_ref[0:1, :]
    for j in range(1, CONV_W):
        xc = xc + xext_ref[halo - 3 + j:halo - 3 + j + tt, :] * cw_ref[j:j + 1, :]
    xc_ref[...] = xc
    _rglru_gates(xc_ref, a_ref, u_ref, wa_ref, wx_ref, ba_ref, bx_ref, lam_ref, n_blocks, bw)

    def step(t, h):
        row = pl.ds(t, 1)
        h = a_ref[row, :] * h + u_ref[row, :]
        u_ref[row, :] = h
        return h

    h = lax.fori_loop(0, tt, step, hc_ref[...], unroll=8)
    hc_ref[...] = h
    y_ref[...] = (u_ref[...] * jax.nn.gelu(xg_ref[...])).astype(y_ref.dtype)

    @pl.when(i == pl.num_programs(1) - 1)
    def _():
        hl_ref[...] = h
        nb_ref[...] = xext_ref[tt:tt + halo, :]


def _rglru_seq(xrg, buf8, h0, lw, layer, *, B, T, DR):
    tt = min(T, 256)
    nt = T // tt
    n_blocks, bw = lw["w_rg_a"].shape[1:3]

    def per_layer(shape):
        return pl.BlockSpec((None,) + shape, lambda b, i: (layer,) + (0,) * len(shape))

    per_seq8 = pl.BlockSpec((None, SUBLANES, DR), lambda b, i: (b, 0, 0))
    per_seq1 = pl.BlockSpec((None, 1, DR), lambda b, i: (b, 0, 0))
    body = functools.partial(_rglru_seq_body, tt=tt, n_blocks=n_blocks, bw=bw)
    return pl.pallas_call(
        body, grid=(B, nt),
        in_specs=[pl.BlockSpec((tt, DR), lambda b, i: (b * nt + i, 0)),
                  pl.BlockSpec((tt, DR), lambda b, i: (b * nt + i, 1)),
                  per_seq8, per_seq1,
                  per_layer((CONV_W, DR)), per_layer((1, DR)),
                  per_layer((n_blocks, bw, bw)), per_layer((n_blocks, bw, bw)),
                  per_layer((1, DR)), per_layer((1, DR)), per_layer((1, DR))],
        out_specs=[pl.BlockSpec((tt, DR), lambda b, i: (b * nt + i, 0)), per_seq1, per_seq8],
        out_shape=[jax.ShapeDtypeStruct((B * T, DR), BF16), jax.ShapeDtypeStruct((B, 1, DR), F32),
                   jax.ShapeDtypeStruct((B, SUBLANES, DR), F32)],
        scratch_shapes=[pltpu.VMEM((tt + SUBLANES, DR), F32), pltpu.VMEM((tt, DR), F32),
                        pltpu.VMEM((tt, DR), F32), pltpu.VMEM((tt, DR), F32), pltpu.VMEM((1, DR), F32)],
        compiler_params=_params(2), name="rglru_seq",
    )(xrg, xrg, buf8, h0, lw["conv_w"], lw["conv_b"], lw["w_rg_a"], lw["w_rg_x"],
      lw["b_rg_a"], lw["b_rg_x"], lw["rg_lambda"])


def _rglru_step_body(xr_ref, xg_ref, buf_ref, h0_ref, cw_ref, cb_ref, wa_ref, wx_ref, ba_ref, bx_ref, lam_ref,
                     y_ref, hl_ref, nb_ref, xc_ref, a_ref, u_ref, *, n_blocks, bw):
    xr = xr_ref[...]
    xc = cb_ref[...] + xr * cw_ref[CONV_W - 1:CONV_W, :]
    for j in range(CONV_W - 1):
        xc = xc + buf_ref[j] * cw_ref[j:j + 1, :]
    xc_ref[...] = xc
    _rglru_gates(xc_ref, a_ref, u_ref, wa_ref, wx_ref, ba_ref, bx_ref, lam_ref, n_blocks, bw)
    h = a_ref[...] * h0_ref[...] + u_ref[...]
    hl_ref[...] = h
    y_ref[...] = h * jax.nn.gelu(xg_ref[...])
    for j in range(CONV_W - 2):
        nb_ref[j] = buf_ref[j + 1]
    nb_ref[CONV_W - 2] = xr


def _rglru_step(xrg, bufT, h0, lw, layer, *, DB, DR):
    n_blocks, bw = lw["w_rg_a"].shape[1:3]

    def per_layer(shape):
        return pl.BlockSpec((None,) + shape, lambda i: (layer,) + (0,) * len(shape))

    tile = pl.BlockSpec((DB, DR), lambda i: (0, 0))
    body = functools.partial(_rglru_step_body, n_blocks=n_blocks, bw=bw)
    return pl.pallas_call(
        body, grid=(1,),
        in_specs=[tile, pl.BlockSpec((DB, DR), lambda i: (0, 1)),
                  per_layer((CONV_W - 1, DB, DR)), per_layer((DB, DR)),
                  per_layer((CONV_W, DR)), per_layer((1, DR)),
                  per_layer((n_blocks, bw, bw)), per_layer((n_blocks, bw, bw)),
                  per_layer((1, DR)), per_layer((1, DR)), per_layer((1, DR))],
        out_specs=[tile, tile, pl.BlockSpec((CONV_W - 1, DB, DR), lambda i: (0, 0, 0))],
        out_shape=[jax.ShapeDtypeStruct((DB, DR), F32), jax.ShapeDtypeStruct((DB, DR), F32),
                   jax.ShapeDtypeStruct((CONV_W - 1, DB, DR), F32)],
        scratch_shapes=[pltpu.VMEM((DB, DR), F32)] * 3,
        compiler_params=_params(1), name="rglru_step",
    )(xrg, xrg, bufT, h0, lw["conv_w"], lw["conv_b"], lw["w_rg_a"], lw["w_rg_x"],
      lw["b_rg_a"], lw["b_rg_x"], lw["rg_lambda"])


def _merge_body(ya_ref, yr_ref, wa_ref, wr_ref, ga_ref, gr_ref, o_ref, wab_ref, wrb_ref):
    @pl.when(pl.program_id(1) == 0)
    def _():
        wab_ref[...] = wa_ref[...].astype(BF16)
        wrb_ref[...] = wr_ref[...].astype(BF16)

    za = jnp.dot(ya_ref[...], wab_ref[...], preferred_element_type=F32)
    zr = jnp.dot(yr_ref[...], wrb_ref[...], preferred_element_type=F32)
    o_ref[...] = (jax.nn.sigmoid(ga_ref[...]) * za + jax.nn.sigmoid(gr_ref[...]) * zr).astype(o_ref.dtype)


def _merge(ya, yr, w_up_att, w_up_rnn, gates, layer):
    M, att = ya.shape
    dr = yr.shape[1]
    D = w_up_att.shape[-1]
    tm = min(M, 512)
    tn = _pick_tile(D, 0)
    nj = D // tn
    return pl.pallas_call(
        _merge_body, grid=(nj, M // tm),
        in_specs=[pl.BlockSpec((tm, att), lambda j, i: (i, 0)),
                  pl.BlockSpec((tm, dr), lambda j, i: (i, 0)),
                  pl.BlockSpec((None, att, tn), lambda j, i: (layer, 0, j)),
                  pl.BlockSpec((None, dr, tn), lambda j, i: (layer, 0, j)),
                  pl.BlockSpec((tm, tn), lambda j, i: (i, j)),
                  pl.BlockSpec((tm, tn), lambda j, i: (i, nj + j))],
        out_specs=pl.BlockSpec((tm, tn), lambda j, i: (i, j)),
        out_shape=jax.ShapeDtypeStruct((M, D), BF16),
        scratch_shapes=[pltpu.VMEM((att, tn), BF16), pltpu.VMEM((dr, tn), BF16)],
        compiler_params=_params(2), name="merge",
    )(ya, yr, w_up_att, w_up_rnn, gates, gates)


def _mlp_body(h_ref, wi_ref, wo_ref, o_ref):
    f = pl.program_id(1)
    hid = jnp.dot(h_ref[...], wi_ref[...], preferred_element_type=F32)
    hid = jnp.square(jnp.maximum(hid, 0.0)).astype(BF16)
    part = jnp.dot(hid, wo_ref[...], preferred_element_type=F32)

    @pl.when(f == 0)
    def _():
        o_ref[...] = part

    @pl.when(f > 0)
    def _():
        o_ref[...] += part


def _mlp(h, wi, wo, layer):
    M, D = h.shape
    dff = wi.shape[2]
    tm = min(M, 512)
    tf = _pick_tile(dff, 0)
    return pl.pallas_call(
        _mlp_body, grid=(M // tm, dff // tf),
        in_specs=[pl.BlockSpec((tm, D), lambda i, f: (i, 0)),
                  pl.BlockSpec((None, D, tf), lambda i, f: (layer, 0, f)),
                  pl.BlockSpec((None, tf, D), lambda i, f: (layer, f, 0))],
        out_specs=pl.BlockSpec((tm, D), lambda i, f: (i, 0)),
        out_shape=jax.ShapeDtypeStruct((M, D), F32),
        compiler_params=_params(2), name="mlp",
    )(h, wi, wo)


def _silu_bf16(c):
    return (c * jax.nn.sigmoid(c)).astype(BF16)


def _in_projection(h, w_in, wseg, layer, dims):
    att, idw, dr, d = dims["att"], dims["idx_width"], dims["dr"], dims["d"]
    q_scale = dims["q_scale"]
    iq_scale = dims["idx_dim"] ** -0.5
    idx_dim = dims["idx_dim"]
    (q,) = _matmul(h, w_in, layer, 0, att, [(att, BF16)], lambda acc: [acc * q_scale], name="proj_q")
    kv, kvb = _matmul(h, w_in, layer, att, 2 * att, [(2 * att, F32), (2 * att, BF16)],
                      lambda acc: [acc, acc], name="proj_kv")
    (iq,) = _matmul(h, w_in, layer, 3 * att, idw, [(idw, BF16)], lambda acc: [acc * iq_scale], name="proj_iq")
    ikw, ikb = _matmul(h, w_in, layer, 3 * att + idw, 2 * LANES, [(2 * LANES, F32), (idx_dim, BF16)],
                       lambda acc: [acc, acc[:, :idx_dim]], tn=2 * LANES, name="proj_ikw")
    w_tail, tail_layer, tail0 = wseg["tail"]
    (xrg,) = _matmul(h, w_tail, tail_layer, tail0, 2 * dr, [(2 * dr, F32)], lambda acc: [acc], name="proj_rg")
    (gates,) = _matmul(h, w_tail, tail_layer, tail0 + 2 * dr, 2 * d, [(2 * d, F32)], lambda acc: [acc],
                       name="proj_gate")
    return q, kv, kvb, iq, ikw, ikb, xrg, gates


def _finish_layer(x, ya, yr, gates, st, st_next, lw, wseg, norm_g3, layer):
    d = x.shape[1]
    merged = _merge(ya, yr, lw["w_up_att"], lw["w_up_rnn"], gates, layer)
    (y,) = _matmul(merged, lw["w_o"], layer, 0, d, [(d, F32)], lambda acc: [acc], name="proj_o")
    x, h2 = _postnorm(x, y, norm_g3, 4 * layer + 1, st, 2, nxt=(4 * layer + 2, st, 3, 4))
    ff = _mlp(h2, wseg["mlp_in"], wseg["mlp_out"], layer)
    nxt = None if st_next is None else (4 * (layer + 1), st_next, 0, 1)
    return _postnorm(x, ff, norm_g3, 4 * layer + 3, st, 5, nxt=nxt)


def kernel(x_prompt, x_sample, cache_k, cache_v, cache_idx_k, state_rglru_h, state_conv, page_table, c_prompt, c_sample, w_ada, b_ada, norm_g, w_in, rel_bias, conv_w, conv_b, w_rg_a, b_rg_a, w_rg_x, b_rg_x, rg_lambda, w_up_att, w_up_rnn, w_o, w_mlp_in, w_mlp_out):
    B, T, D = x_prompt.shape
    DB, t_new, _ = x_sample.shape
    assert t_new == 1, "the sample group decodes one token per sequence"
    L = w_ada.shape[0]
    n_pool, page, n_heads, head_dim = cache_k.shape[1:]
    att = n_heads * head_dim
    idx_dim = cache_idx_k.shape[-1]
    DR = state_rglru_h.shape[-1]
    in_w = w_in.shape[-1]
    idx_heads = (in_w - 3 * att - idx_dim - 2 * DR - 2 * D) // (idx_dim + 1)
    idw = idx_heads * idx_dim
    n_pages = page_table.shape[1]
    past = n_pages * page
    assert idx_dim == LANES and idx_heads <= LANES and n_heads <= LANES and page == LANES
    dims = dict(att=att, idx_width=idw, dr=DR, d=D, idx_dim=idx_dim, q_scale=head_dim ** -0.5 * LOG2_E)
    o_ik = 3 * att + idw
    o_xr = o_ik + idx_dim + idx_heads
    assert o_ik % (2 * LANES) == 0 and o_xr + 2 * DR + 2 * D == in_w

    tq = min(T, 256)
    nq = T // tq
    topk_p = min(TOPK_MAX, T // 4)
    topk_s = min(TOPK_MAX, (past + 1) // 4)
    Mp = B * T
    Ms = 2 * SUBLANES
    assert DB <= Ms and B + Ms <= 4 * SUBLANES and tq % LANES == 0 and T % tq == 0 and tq >= MAX_DISTANCE

    c_all = jnp.zeros((4 * SUBLANES, D), F32).at[:B].set(c_prompt).at[B:B + DB].set(c_sample)
    b_ada3 = b_ada.reshape(L, 1, 6 * D)
    norm_g3 = norm_g.reshape(L * 4, 1, D)
    lw_all = dict(conv_w=conv_w, conv_b=conv_b.reshape(L, 1, DR), w_rg_a=w_rg_a, w_rg_x=w_rg_x,
                  b_rg_a=b_rg_a.reshape(L, 1, DR), b_rg_x=b_rg_x.reshape(L, 1, DR),
                  rg_lambda=rg_lambda.reshape(L, 1, DR), w_up_att=w_up_att, w_up_rnn=w_up_rnn, w_o=w_o)

    bias_tiles = _bias_tiles(rel_bias, tq)
    rel_pad = jnp.pad(rel_bias, ((0, 0), (0, LANES - n_heads)))
    mlp_in16 = w_mlp_in.astype(BF16)
    mlp_out16 = w_mlp_out.astype(BF16)
    conv_sT = jnp.swapaxes(state_conv, 1, 2)
    zero_buf8 = jnp.zeros((B, SUBLANES, DR), F32)
    zero_h = jnp.zeros((B, 1, DR), F32)

    xp = x_prompt.reshape(Mp, D)
    xs = jnp.pad(x_sample.reshape(DB, D), ((0, Ms - DB), (0, 0)))
    hp = hs = None
    outs_p = [[] for _ in range(5)]
    outs_s = [[] for _ in range(5)]

    tn_ada = _pick_tile(6 * D, 0)
    streams_p, streams_s = [], []
    for l in range(L):
        (mod,) = _matmul(c_all, w_ada, l, 0, 6 * D, [(6 * D, F32)], lambda acc, b: [acc + b],
                         extras=[(b_ada3, pl.BlockSpec((None, 1, tn_ada), lambda j, i, l=l: (l, 0, j)))],
                         a_fn=_silu_bf16, name="ada")
        streams_p.append(_Stream(mod[:B].reshape(B, 1, 6 * D), T, Mp, D))
        streams_s.append(_Stream(mod[B:B + Ms], 1, Ms, D))
    streams_p.append(None)
    streams_s.append(None)

    for l in range(L):
        st_p, st_s = streams_p[l], streams_s[l]
        if o_xr % LANES:
            tail = (_repack_columns(w_in, l, o_xr, 2 * DR + 2 * D)[None], 0, 0)
        else:
            tail = (w_in, l, o_xr)
        wseg = dict(tail=tail, mlp_in=mlp_in16, mlp_out=mlp_out16)
        if l == 0:
            hp = _prenorm(xp, norm_g3, 0, st_p, 0, 1)
            hs = _prenorm(xs, norm_g3, 0, st_s, 0, 1)

        q, kv, kvb, iq, ikw, ikb, xrg, gates = _in_projection(hp, w_in, wseg, l, dims)
        qT = jnp.swapaxes(q.reshape(B * nq, tq, att), 1, 2)
        iqT = jnp.swapaxes(iq.reshape(B * nq, tq, idw), 1, 2)
        vT = jnp.swapaxes(kvb[:, att:].reshape(B * nq, tq, att), 1, 2)
        wT = jnp.swapaxes(ikw[:, idx_dim:].reshape(B * nq, tq, LANES), 1, 2)
        yT = _dsa_prompt(qT, kvb, vT, iqT, ikb, wT, bias_tiles, B=B, T=T, tq=tq, topk=topk_p,
                         n_heads=n_heads, head_dim=head_dim, idx_heads=idx_heads, idx_dim=idx_dim)
        ya = jnp.swapaxes(yT, 1, 2).reshape(Mp, att)
        yr, h_last, nbuf = _rglru_seq(xrg, zero_buf8, zero_h, lw_all, l, B=B, T=T, DR=DR)
        xp, hp = _finish_layer(xp, ya, yr, gates, st_p, streams_p[l + 1], lw_all, wseg, norm_g3, l)
        outs_p[0].append(kv[:, :att].reshape(B, T, n_heads, head_dim))
        outs_p[1].append(kv[:, att:].reshape(B, T, n_heads, head_dim))
        outs_p[2].append(ikw[:, :idx_dim].reshape(B, T, idx_dim))
        outs_p[3].append(h_last.reshape(B, DR))
        outs_p[4].append(nbuf[:, SUBLANES - (CONV_W - 1):])

        q, kv, kvb, iq, ikw, ikb, xrg, gates = _in_projection(hs, w_in, wseg, l, dims)
        k_new = kv[:DB, :att]
        v_new = kv[:DB, att:]
        ik_new = ikw[:DB, :idx_dim]
        iq3 = iq[:DB].reshape(DB, idx_heads, idx_dim)
        w3 = jnp.broadcast_to(ikw[:DB, idx_dim:idx_dim + idx_heads, None], (DB, idx_heads, page))
        new_pages = jnp.zeros((DB, page, idx_dim), F32).at[:, 0].set(ik_new)
        scores = _sample_scores(page_table, iq3, w3, cache_idx_k, new_pages, l)
        idx = _sample_topk(scores.reshape(DB, -1), topk_s)
        q_f32 = q[:DB].astype(F32) / dims["q_scale"]
        ya_s = _sample_attend(idx, page_table, q_f32.reshape(DB, n_heads, head_dim),
                              k_new.reshape(DB, n_heads, head_dim), v_new.reshape(DB, n_heads, head_dim),
                              rel_pad, cache_k, cache_v, l)
        ya_s = jnp.pad(ya_s.reshape(DB, att), ((0, Ms - DB), (0, 0))).astype(BF16)
        y_s, h_new, nbufT = _rglru_step(xrg, conv_sT, state_rglru_h, lw_all, l, DB=DB, DR=DR)
        yr_s = jnp.pad(y_s, ((0, Ms - DB), (0, 0))).astype(BF16)
        xs, hs = _finish_layer(xs, ya_s, yr_s, gates, st_s, streams_s[l + 1], lw_all, wseg, norm_g3, l)
        outs_s[0].append(k_new.reshape(DB, 1, n_heads, head_dim))
        outs_s[1].append(v_new.reshape(DB, 1, n_heads, head_dim))
        outs_s[2].append(ik_new.reshape(DB, 1, idx_dim))
        outs_s[3].append(h_new)
        outs_s[4].append(jnp.swapaxes(nbufT, 0, 1))

    y_prompt = xp.reshape(B, T, D)
    y_sample = xs[:DB].reshape(DB, 1, D)
    return (y_prompt, y_sample, *[jnp.stack(o) for o in outs_p], *[jnp.stack(o) for o in outs_s])
```

```python
import functools
import math

import jax
import jax.numpy as jnp
from jax import lax
from jax.experimental import pallas as pl
from jax.experimental.pallas import tpu as pltpu

F32 = jnp.float32
BF16 = jnp.bfloat16
I32 = jnp.int32

V7X_VMEM_BYTES = 64 * 1024 * 1024
VMEM_LIMIT = V7X_VMEM_BYTES - 8 * 1024 * 1024
LANES = 128
SUBLANES = 8

TOPK_MAX = 256
MAX_DISTANCE = 128
RG_C = 8.0
NORM_EPS = 1e-6
CONV_W = 4
MASK_NEG = -1e30
LOG2_E = math.log2(math.e)
ATTN_HEAD_GROUP = 4
INT_MIN = -2 ** 31
INT_MAX = 2 ** 31 - 1


def _params(n_axes):
    return pltpu.CompilerParams(dimension_semantics=("arbitrary",) * n_axes, vmem_limit_bytes=VMEM_LIMIT)


def _pick_tile(n, col0, candidates=(512, 256, 128)):
    for t in candidates:
        if n % t == 0 and col0 % t == 0:
            return t
    raise ValueError(f"no lane tile for width {n} at column {col0}")


def _mm_body(*refs, n_extra, n_out, cast_w, w_t, has2, a_fn, epilogue):
    a_ref, w_ref = refs[0], refs[1]
    pos = 2
    a2_ref = None
    if has2:
        a2_ref = refs[pos]
        pos += 1
    extra = refs[pos:pos + n_extra]
    pos += n_extra
    outs = refs[pos:pos + n_out]
    pos += n_out
    outs2 = refs[pos:pos + n_out] if has2 else ()
    pos += len(outs2)
    first_row_tile = pl.program_id(1) == 0
    if cast_w:
        wb_ref = refs[pos]

        @pl.when(first_row_tile)
        def _():
            wb_ref[...] = (w_ref[0] if w_t else w_ref[...]).astype(BF16)

        w_ref = wb_ref

    def project(a):
        if a_fn is not None:
            a = a_fn(a)
        contract = (((1,), (1,)), ((), ())) if w_t else (((1,), (0,)), ((), ()))
        acc = lax.dot_general(a, w_ref[...], contract, preferred_element_type=F32)
        return epilogue(acc, *[e[...] for e in extra])

    for o_ref, r in zip(outs, project(a_ref[...])):
        o_ref[...] = r.astype(o_ref.dtype)
    if has2:
        @pl.when(first_row_tile)
        def _():
            for o_ref, r in zip(outs2, project(a2_ref[...])):
                o_ref[...] = r.astype(o_ref.dtype)


def _matmul(a, w, layer, col0, n, outs, epilogue, *, a2=None, extras=(), a_fn=None, tn=None, w_t=False,
            name="mm"):
    M, K = a.shape
    tm = min(M, 1024)
    if tn is None:
        tn = _pick_tile(n, 0 if w_t else col0)
    assert M % tm == 0 and n % tn == 0
    cast_w = w.dtype != BF16
    if w_t:
        assert col0 % SUBLANES == 0 and cast_w
        w_spec = pl.BlockSpec((pl.Element(1), pl.Element(tn), pl.Element(K)),
                              lambda j, i: (layer, pl.multiple_of(col0 + j * tn, SUBLANES), 0))
    else:
        assert col0 % tn == 0
        w_spec = pl.BlockSpec((None, K, tn), lambda j, i: (layer, 0, col0 // tn + j))
    in_specs = [pl.BlockSpec((tm, K), lambda j, i: (i, 0)), w_spec]
    args = [a, w]
    out_specs = [pl.BlockSpec((tm, tn * wd // n), lambda j, i: (i, j)) for wd, _ in outs]
    out_shape = [jax.ShapeDtypeStruct((M, wd), dt) for wd, dt in outs]
    if a2 is not None:
        m2 = a2.shape[0]
        in_specs.append(pl.BlockSpec((m2, K), lambda j, i: (0, 0)))
        args.append(a2)
        out_specs += [pl.BlockSpec((m2, tn * wd // n), lambda j, i: (0, j)) for wd, _ in outs]
        out_shape += [jax.ShapeDtypeStruct((m2, wd), dt) for wd, dt in outs]
    in_specs += [spec for _, spec in extras]
    args += [arr for arr, _ in extras]
    body = functools.partial(_mm_body, n_extra=len(extras), n_out=len(outs), cast_w=cast_w, w_t=w_t,
                             has2=a2 is not None, a_fn=a_fn, epilogue=epilogue)
    res = pl.pallas_call(
        body, grid=(n // tn, M // tm), in_specs=in_specs, out_specs=out_specs, out_shape=out_shape,
        scratch_shapes=[pltpu.VMEM((tn, K) if w_t else (K, tn), BF16)] if cast_w else [],
        compiler_params=_params(2), name=name,
    )(*args)
    return res if a2 is None else (res[:len(outs)], res[len(outs):])


def _mm_ksplit_body(a_ref, w_ref, a2_ref, o_ref, o2_ref, wb_ref, *, tm):
    k = pl.program_id(2)
    i = pl.program_id(3)

    @pl.when(i == 0)
    def _():
        wb_ref[...] = w_ref[...].astype(BF16)

    def accumulate(dst, rows, a):
        part = jnp.dot(a, wb_ref[...], preferred_element_type=F32)

        @pl.when(k == 0)
        def _():
            dst[rows, :] = part

        @pl.when(k > 0)
        def _():
            dst[rows, :] += part

    accumulate(o_ref, pl.ds(pl.multiple_of(i * tm, tm), tm), a_ref[...])

    @pl.when(i == 0)
    def _():
        accumulate(o2_ref, slice(None), a2_ref[...])


def _matmul_ksplit(a, a2, w, layer):
    M, K = a.shape
    m2 = a2.shape[0]
    N = w.shape[2]
    tm = min(M, 1024)
    tk = min(K, 2048)
    tn = _pick_tile(N, 0)
    n_panels = max(1, (M * tn * 4) // (8 * 1024 * 1024))
    panel = M // n_panels
    assert M % tm == 0 and K % tk == 0 and panel % tm == 0
    ni = panel // tm
    res = pl.pallas_call(
        functools.partial(_mm_ksplit_body, tm=tm), grid=(n_panels, N // tn, K // tk, ni),
        in_specs=[pl.BlockSpec((tm, tk), lambda p, j, k, i: (p * ni + i, k)),
                  pl.BlockSpec((None, tk, tn), lambda p, j, k, i: (layer, k, j)),
                  pl.BlockSpec((m2, tk), lambda p, j, k, i: (0, k))],
        out_specs=[pl.BlockSpec((panel, tn), lambda p, j, k, i: (p, j)),
                   pl.BlockSpec((m2, tn), lambda p, j, k, i: (p, j))],
        out_shape=[jax.ShapeDtypeStruct((M, N), F32), jax.ShapeDtypeStruct((n_panels * m2, N), F32)],
        scratch_shapes=[pltpu.VMEM((tk, tn), BF16)],
        compiler_params=_params(4), name="mm_ksplit",
    )(a, w, a2)
    return res[0], res[1][:m2]


def _rms(x, g):
    return x * lax.rsqrt(jnp.mean(x * x, axis=-1, keepdims=True) + NORM_EPS) * g


def _prenorm_body(x_ref, g_ref, sh_ref, sc_ref, h_ref):
    h = _rms(x_ref[...], g_ref[...]) * (1.0 + sc_ref[...]) + sh_ref[...]
    h_ref[...] = h.astype(h_ref.dtype)


def _postnorm_body(*refs, with_next):
    if with_next:
        x_ref, y_ref, gpost_ref, gate_ref, gpre_ref, sh_ref, sc_ref, xo_ref, h_ref = refs
    else:
        x_ref, y_ref, gpost_ref, gate_ref, xo_ref = refs
    x = x_ref[...] + gate_ref[...] * _rms(y_ref[...], gpost_ref[...])
    xo_ref[...] = x
    if with_next:
        h = _rms(x, gpre_ref[...]) * (1.0 + sc_ref[...]) + sh_ref[...]
        h_ref[...] = h.astype(h_ref.dtype)


class _Stream:
    def __init__(self, mod, rows_per_seq, n_rows, d):
        self.mod = mod
        self.rows_per_seq = rows_per_seq
        self.tm = min(256, n_rows)
        self.d = d

    def mod_spec(self, chunk):
        tm, d = self.tm, self.d
        if self.rows_per_seq > 1:
            rps = self.rows_per_seq
            return pl.BlockSpec((None, 1, d), lambda i: (i * tm // rps, 0, chunk))
        return pl.BlockSpec((tm, d), lambda i: (i, chunk))


def _norm_spec(g_index, d):
    return pl.BlockSpec((None, 1, d), lambda i: (g_index, 0, 0))


def _prenorm(x, norm_g3, g_index, st, shift_chunk, scale_chunk):
    M, D = x.shape
    tm = st.tm
    row = pl.BlockSpec((tm, D), lambda i: (i, 0))
    return pl.pallas_call(
        _prenorm_body, grid=(M // tm,),
        in_specs=[row, _norm_spec(g_index, D), st.mod_spec(shift_chunk), st.mod_spec(scale_chunk)],
        out_specs=row, out_shape=jax.ShapeDtypeStruct((M, D), BF16),
        compiler_params=_params(1), name="prenorm",
    )(x, norm_g3, st.mod, st.mod)


def _postnorm(x, y, norm_g3, g_post, st, gate_chunk, nxt=None):
    M, D = x.shape
    tm = st.tm
    row = pl.BlockSpec((tm, D), lambda i: (i, 0))
    in_specs = [row, row, _norm_spec(g_post, D), st.mod_spec(gate_chunk)]
    args = [x, y, norm_g3, st.mod]
    out_specs = [row]
    out_shape = [jax.ShapeDtypeStruct((M, D), F32)]
    if nxt is not None:
        g_pre, st_n, sh_c, sc_c = nxt
        in_specs += [_norm_spec(g_pre, D), st_n.mod_spec(sh_c), st_n.mod_spec(sc_c)]
        args += [norm_g3, st_n.mod, st_n.mod]
        out_specs.append(row)
        out_shape.append(jax.ShapeDtypeStruct((M, D), BF16))
    res = pl.pallas_call(
        functools.partial(_postnorm_body, with_next=nxt is not None), grid=(M // tm,),
        in_specs=in_specs, out_specs=out_specs, out_shape=out_shape,
        compiler_params=_params(1), name="postnorm",
    )(*args)
    return res if nxt is not None else (res[0], None)


def _t5_bucket(dist, n_buckets):
    max_exact = n_buckets // 2
    d = jnp.maximum(dist, 0)
    d_f = jnp.maximum(d, 1).astype(F32)
    large = max_exact + (jnp.log(d_f / max_exact) / math.log(MAX_DISTANCE / max_exact)
                         * (n_buckets - max_exact)).astype(I32)
    large = jnp.minimum(large, n_buckets - 1)
    return jnp.where(d < max_exact, d, large)


def _bias_tiles_body(rel_ref, o_ref, *, tq, n_buckets):
    which = pl.program_id(0)
    h = pl.program_id(1)
    r = lax.broadcasted_iota(I32, (tq, tq), 0)
    c = lax.broadcasted_iota(I32, (tq, tq), 1)
    bucket = _t5_bucket(which * tq + c - r, n_buckets)
    far = rel_ref[n_buckets - 1, h]
    acc = jnp.zeros((tq, tq), F32)
    for n in range(n_buckets - 1):
        acc = jnp.where(bucket == n, rel_ref[n, h] - far, acc)
    o_ref[...] = acc * LOG2_E


def _bias_tiles(rel_bias, tq):
    nb, n_heads = rel_bias.shape
    return pl.pallas_call(
        functools.partial(_bias_tiles_body, tq=tq, n_buckets=nb), grid=(2, n_heads),
        in_specs=[pl.BlockSpec(memory_space=pltpu.SMEM)],
        out_specs=pl.BlockSpec((None, None, tq, tq), lambda w, h: (w, h, 0, 0)),
        out_shape=jax.ShapeDtypeStruct((2, n_heads, tq, tq), F32),
        compiler_params=_params(2), name="bias_tiles",
    )(rel_bias)


def _attn_body(qT_ref, k_ref, vT_ref, iqT_ref, ik_ref, wT_ref, bias_ref, o_ref,
               key_ref, mask_ref, p_ref, m_ref, l_ref, acc_ref, lg_ref,
               *, topk, n_heads, head_dim, idx_heads, idx_dim, tq, n_chunks):
    i = pl.program_id(1)
    nck = i + 1
    w_scale = idx_heads ** -0.5
    q_pos = i * tq + lax.broadcasted_iota(I32, (tq, tq), 1)
    r_loc = lax.broadcasted_iota(I32, (tq, tq), 0)

    def rows(c):
        return pl.ds(pl.multiple_of(c * tq, tq), tq)

    def score_chunk(c, carry):
        ik_c = ik_ref[rows(c), :]
        sc = None
        for h in range(idx_heads):
            s = jnp.dot(ik_c, iqT_ref[h * idx_dim:(h + 1) * idx_dim, :], preferred_element_type=F32)
            contrib = jnp.maximum(s, 0.0) * (wT_ref[h:h + 1, :] * w_scale)
            sc = contrib if sc is None else sc + contrib
        bits = pltpu.bitcast(sc + 0.0, I32)
        key = jnp.where(bits < 0, bits ^ INT_MAX, bits)
        key_ref[rows(c), :] = jnp.where(c * tq + r_loc <= q_pos, key, INT_MIN)
        return carry

    lax.fori_loop(0, nck, score_chunk, 0)

    def count(pred):
        def body(c, cnt):
            hit = pred(key_ref[rows(c), :], c).astype(I32).reshape(tq // SUBLANES, SUBLANES, tq)
            return cnt + jnp.sum(hit, axis=0)
        cnt = lax.fori_loop(0, nck, body, jnp.zeros((SUBLANES, tq), I32))
        return jnp.sum(cnt, axis=0, keepdims=True)

    thr = jnp.where(count(lambda k, c: k >= 0) >= topk, 0, INT_MIN).astype(I32)

    def thr_bit(s, thr):
        cand = thr | jnp.left_shift(jnp.int32(1), 30 - s)
        return jnp.where(count(lambda k, c: k >= cand) >= topk, cand, thr)

    thr = lax.fori_loop(0, 31, thr_bit, thr)

    n_gt = count(lambda k, c: k > thr)
    n_ge = count(lambda k, c: k >= thr)
    need = topk - n_gt
    p_ref[...] = jnp.full((1, tq), INT_MAX, I32)
    has_ties = jnp.max(jnp.where((n_ge > topk) & (thr > INT_MIN), 1, 0)) > 0

    @pl.when(has_ties)
    def _():
        n_bits = max(1, (n_chunks * tq - 1).bit_length())

        def pos_bit(s, p):
            cand = p | jnp.left_shift(jnp.int32(1), n_bits - 1 - s)
            below = count(lambda k, c: (k == thr) & (c * tq + r_loc < cand))
            return jnp.where(below < need, cand, p)

        p_ref[...] = lax.fori_loop(0, n_bits, pos_bit, jnp.zeros((1, tq), I32))

    p_last = p_ref[...]

    def mask_chunk(c, carry):
        k = key_ref[rows(c), :]
        pos = c * tq + r_loc
        keep = ((k > thr) | ((k == thr) & (pos <= p_last))) & (pos <= q_pos)
        mask_ref[c] = jnp.where(keep, 0.0, MASK_NEG)
        return carry

    lax.fori_loop(0, nck, mask_chunk, 0)

    m_ref[...] = jnp.full(m_ref.shape, MASK_NEG, F32)
    l_ref[...] = jnp.zeros(l_ref.shape, F32)
    acc_ref[...] = jnp.zeros(acc_ref.shape, F32)
    group = lg_ref.shape[0]

    def logits(c, carry, h0, near):
        mask_c = mask_ref[c]
        for g in range(group):
            h = h0 + g
            hs = slice(h * head_dim, (h + 1) * head_dim)
            s = jnp.dot(k_ref[rows(c), hs], qT_ref[hs, :], preferred_element_type=F32) + mask_c
            if near:
                s = s + bias_ref[i - c, h]
            lg_ref[g, rows(c), :] = s
            m_ref[h] = jnp.maximum(m_ref[h], jnp.max(s, axis=0, keepdims=True))
        return carry

    def values(c, carry, h0):
        for g in range(group):
            h = h0 + g
            hs = slice(h * head_dim, (h + 1) * head_dim)
            p = jnp.exp2(lg_ref[g, rows(c), :] - m_ref[h])
            l_ref[h] += jnp.sum(p, axis=0, keepdims=True)
            acc_ref[h] += jnp.dot(vT_ref[c, hs, :], p.astype(BF16), preferred_element_type=F32)
        return carry

    c_near = jnp.maximum(i - 1, 0)
    for h0 in range(0, n_heads, group):
        lax.fori_loop(0, c_near, functools.partial(logits, h0=h0, near=False), 0)
        lax.fori_loop(c_near, nck, functools.partial(logits, h0=h0, near=True), 0)
        lax.fori_loop(0, nck, functools.partial(values, h0=h0), 0)
    for h in range(n_heads):
        hs = slice(h * head_dim, (h + 1) * head_dim)
        o_ref[hs, :] = (acc_ref[h] / l_ref[h]).astype(o_ref.dtype)


def _dsa_prompt(qT, kb, vT, iqT, ikb, wT, bias, *, B, T, tq, topk, n_heads, head_dim, idx_heads, idx_dim):
    nq = T // tq
    att = n_heads * head_dim
    once = dict(pipeline_mode=pl.Buffered(1))
    body = functools.partial(_attn_body, topk=topk, n_heads=n_heads, head_dim=head_dim,
                             idx_heads=idx_heads, idx_dim=idx_dim, tq=tq, n_chunks=nq)
    return pl.pallas_call(
        body, grid=(B, nq),
        in_specs=[
            pl.BlockSpec((None, att, tq), lambda b, i: (b * nq + i, 0, 0)),
            pl.BlockSpec((T, att), lambda b, i: (b, 0), **once),
            pl.BlockSpec((nq, att, tq), lambda b, i: (b, 0, 0), **once),
            pl.BlockSpec((None, idx_heads * idx_dim, tq), lambda b, i: (b * nq + i, 0, 0)),
            pl.BlockSpec((T, idx_dim), lambda b, i: (b, 0), **once),
            pl.BlockSpec((None, wT.shape[1], tq), lambda b, i: (b * nq + i, 0, 0)),
            pl.BlockSpec((2, n_heads, tq, tq), lambda b, i: (0, 0, 0, 0), **once),
        ],
        out_specs=pl.BlockSpec((None, att, tq), lambda b, i: (b * nq + i, 0, 0)),
        out_shape=jax.ShapeDtypeStruct((B * nq, att, tq), BF16),
        scratch_shapes=[pltpu.VMEM((T, tq), I32), pltpu.VMEM((nq, tq, tq), F32), pltpu.VMEM((1, tq), I32),
                        pltpu.VMEM((n_heads, 1, tq), F32), pltpu.VMEM((n_heads, 1, tq), F32),
                        pltpu.VMEM((n_heads, head_dim, tq), F32),
                        pltpu.VMEM((math.gcd(ATTN_HEAD_GROUP, n_heads), T, tq), F32)],
        compiler_params=_params(2), name="dsa_prompt",
    )(qT, kb, vT, iqT, ikb, wT, bias)


def _sample_scores_body(pt_ref, iq_ref, w_ref, *refs, n_groups, idx_heads):
    page_refs, new_ref, o_ref = refs[:SUBLANES], refs[SUBLANES], refs[SUBLANES + 1]
    j = pl.program_id(1)

    def page_scores(keys):
        s = lax.dot_general(iq_ref[...], keys.astype(BF16), (((1,), (1,)), ((), ())),
                            preferred_element_type=F32)
        w = w_ref[...] * idx_heads ** -0.5
        return jnp.sum(jnp.maximum(s, 0.0) * w, axis=0, keepdims=True) + 0.0

    group = pl.ds(pl.multiple_of(j * SUBLANES, SUBLANES), SUBLANES)

    @pl.when(j < n_groups)
    def _():
        o_ref[group, :] = jnp.concatenate([page_scores(r[...]) for r in page_refs], axis=0)

    @pl.when(j == n_groups)
    def _():
        sc = jnp.broadcast_to(page_scores(new_ref[...]), (SUBLANES, new_ref.shape[0]))
        lane = lax.broadcasted_iota(I32, sc.shape, 1)
        row = lax.broadcasted_iota(I32, sc.shape, 0)
        o_ref[group, :] = jnp.where((lane == 0) & (row == 0), sc, -jnp.inf)


def _sample_scores(page_table, iq3, w3, cache_idx_k, new_pages, layer):
    DB, n_pages = page_table.shape
    _, idx_heads, idx_dim = iq3.shape
    page = cache_idx_k.shape[2]
    assert n_pages % SUBLANES == 0
    n_groups = n_pages // SUBLANES
    rows = n_pages + SUBLANES

    def page_spec(r):
        return pl.BlockSpec(
            (None, None, page, idx_dim),
            lambda b, j, pt: (layer, pt[b, jnp.minimum(j, n_groups - 1) * SUBLANES + r], 0, 0))

    grid_spec = pltpu.PrefetchScalarGridSpec(
        num_scalar_prefetch=1, grid=(DB, n_groups + 1),
        in_specs=[pl.BlockSpec((None, idx_heads, idx_dim), lambda b, j, pt: (b, 0, 0)),
                  pl.BlockSpec((None, idx_heads, page), lambda b, j, pt: (b, 0, 0))]
        + [page_spec(r) for r in range(SUBLANES)]
        + [pl.BlockSpec((None, page, idx_dim), lambda b, j, pt: (b, 0, 0))],
        out_specs=pl.BlockSpec((None, rows, page), lambda b, j, pt: (b, 0, 0)),
    )
    return pl.pallas_call(
        functools.partial(_sample_scores_body, n_groups=n_groups, idx_heads=idx_heads),
        grid_spec=grid_spec, out_shape=jax.ShapeDtypeStruct((DB, rows, page), F32),
        compiler_params=_params(2), name="sample_scores",
    )(page_table, iq3, w3, *([cache_idx_k] * SUBLANES), new_pages)


def _topk_body(s_ref, o_ref, work_ref, *, topk):
    work_ref[...] = s_ref[...]
    lane = lax.broadcasted_iota(I32, s_ref.shape, 1)
    slot = lax.broadcasted_iota(I32, o_ref.shape, 1)

    def body(k, acc):
        s = work_ref[...]
        m = jnp.max(s, axis=1, keepdims=True)
        idx = jnp.min(jnp.where(s == m, lane, INT_MAX), axis=1, keepdims=True)
        work_ref[...] = jnp.where(lane == idx, -jnp.inf, s)
        return jnp.where(slot == k, idx, acc)

    o_ref[...] = lax.fori_loop(0, topk, body, jnp.zeros(o_ref.shape, I32))


def _sample_topk(scores2d, topk):
    DB, n = scores2d.shape
    return pl.pallas_call(
        functools.partial(_topk_body, topk=topk),
        out_shape=jax.ShapeDtypeStruct((DB, topk), I32),
        scratch_shapes=[pltpu.VMEM((DB, n), F32)],
        compiler_params=pltpu.CompilerParams(vmem_limit_bytes=VMEM_LIMIT), name="sample_topk",
    )(scores2d)


def _sample_attend_body(idx_sm, pt_sm, q_ref, knew_ref, vnew_ref, idxc_ref, rel_ref, ck_hbm, cv_hbm, o_ref,
                        kbuf, vbuf, sem, *, layer, past, page, topk, n_heads, head_dim, n_buckets):
    b = pl.program_id(0)

    def row_copies(k):
        p = jnp.minimum(idx_sm[b, k], past - 1)
        phys = pt_sm[b, p // page]
        off = p % page
        return (pltpu.make_async_copy(ck_hbm.at[layer, phys, off], kbuf.at[:, k, :], sem.at[0]),
                pltpu.make_async_copy(cv_hbm.at[layer, phys, off], vbuf.at[:, k, :], sem.at[1]))

    def start(k, carry):
        for cp in row_copies(k):
            cp.start()
        return carry

    def wait(k, carry):
        for cp in row_copies(k):
            cp.wait()
        return carry

    lax.fori_loop(0, topk, start, 0)
    lax.fori_loop(0, topk, wait, 0)

    idxc = idxc_ref[...]
    is_new = idxc >= past
    bucket = _t5_bucket(past - idxc, n_buckets)
    bias = jnp.zeros((topk, LANES), F32)
    for n in range(n_buckets):
        bias = jnp.where(bucket == n, rel_ref[n:n + 1, :], bias)
    for h in range(n_heads):
        row_h = slice(h, h + 1)
        ks = jnp.where(is_new, knew_ref[row_h, :], kbuf[h])
        vs = jnp.where(is_new, vnew_ref[row_h, :], vbuf[h])
        s = jnp.sum(ks * q_ref[row_h, :], axis=1, keepdims=True) * head_dim ** -0.5 + bias[:, h:h + 1]
        e = jnp.exp(s - jnp.max(s, axis=0, keepdims=True))
        den = jnp.sum(e, axis=0, keepdims=True)
        o_ref[row_h, :] = jnp.sum(e * vs, axis=0, keepdims=True) / den


def _sample_attend(idx, page_table, q3, knew3, vnew3, rel_pad, cache_k, cache_v, layer):
    DB, topk = idx.shape
    page, n_heads, head_dim = cache_k.shape[2:]
    past = page_table.shape[1] * page
    row = pl.BlockSpec((None, n_heads, head_dim), lambda b, ix, pt: (b, 0, 0))
    grid_spec = pltpu.PrefetchScalarGridSpec(
        num_scalar_prefetch=2, grid=(DB,),
        in_specs=[row, row, row,
                  pl.BlockSpec((None, topk, 1), lambda b, ix, pt: (b, 0, 0)),
                  pl.BlockSpec(rel_pad.shape, lambda b, ix, pt: (0, 0)),
                  pl.BlockSpec(memory_space=pl.ANY), pl.BlockSpec(memory_space=pl.ANY)],
        out_specs=row,
        scratch_shapes=[pltpu.VMEM((n_heads, topk, head_dim), F32), pltpu.VMEM((n_heads, topk, head_dim), F32),
                        pltpu.SemaphoreType.DMA((2,))],
    )
    body = functools.partial(_sample_attend_body, layer=layer, past=past, page=page, topk=topk,
                             n_heads=n_heads, head_dim=head_dim, n_buckets=rel_pad.shape[0])
    return pl.pallas_call(
        body, grid_spec=grid_spec, out_shape=jax.ShapeDtypeStruct((DB, n_heads, head_dim), F32),
        compiler_params=_params(1), name="sample_attend",
    )(idx, page_table, q3, knew3, vnew3, idx.reshape(DB, topk, 1), rel_pad, cache_k, cache_v)


def _softplus(z):
    return jnp.maximum(z, 0.0) + jnp.log1p(jnp.exp(-jnp.abs(z)))


def _one_minus_exp(x):
    e = jnp.exp(x)
    log_e = jnp.log(e)
    return jnp.where(e == 1.0, -x, (1.0 - e) * x / jnp.where(log_e == 0.0, 1.0, log_e))


def _rglru_gates(xc_ref, a_ref, u_ref, wa_ref, wx_ref, ba_ref, bx_ref, lam_ref, n_blocks, bw):
    for n in range(n_blocks):
        cs = slice(n * bw, (n + 1) * bw)
        xb = xc_ref[:, cs]
        xb16 = xb.astype(BF16)
        r = jax.nn.sigmoid(jnp.dot(xb16, wa_ref[n].astype(BF16), preferred_element_type=F32) + ba_ref[:, cs])
        ig = jax.nn.sigmoid(jnp.dot(xb16, wx_ref[n].astype(BF16), preferred_element_type=F32) + bx_ref[:, cs])
        log_a = -RG_C * r * _softplus(-lam_ref[:, cs])
        a_ref[:, cs] = jnp.exp(log_a)
        u_ref[:, cs] = jnp.sqrt(_one_minus_exp(2.0 * log_a)) * ig * xb


def _rglru_seq_body(xr_ref, xg_ref, buf_ref, h0_ref, cw_ref, cb_ref, wa_ref, wx_ref, ba_ref, bx_ref, lam_ref,
                    y_ref, hl_ref, nb_ref, xext_ref, xc_ref, a_ref, u_ref, hc_ref, *, tt, n_blocks, bw):
    i = pl.program_id(1)
    halo = SUBLANES

    @pl.when(i == 0)
    def _():
        xext_ref[0:halo, :] = buf_ref[...]
        hc_ref[...] = h0_ref[...]

    @pl.when(i > 0)
    def _():
        xext_ref[0:halo, :] = xext_ref[tt:tt + halo, :]

    xext_ref[halo:halo + tt, :] = xr_ref[...]
    xc = cb_ref[...] + xext_ref[halo - 3:halo - 3 + tt, :] * cw_ref[0:1, :]
    for j in range(1, CONV_W):
        xc = xc + xext_ref[halo - 3 + j:halo - 3 + j + tt, :] * cw_ref[j:j + 1, :]
    xc_ref[...] = xc
    _rglru_gates(xc_ref, a_ref, u_ref, wa_ref, wx_ref, ba_ref, bx_ref, lam_ref, n_blocks, bw)

    def step(t, h):
        row = pl.ds(t, 1)
        h = a_ref[row, :] * h + u_ref[row, :]
        u_ref[row, :] = h
        return h

    h = lax.fori_loop(0, tt, step, hc_ref[...], unroll=8)
    hc_ref[...] = h
    y_ref[...] = (u_ref[...] * jax.nn.gelu(xg_ref[...])).astype(y_ref.dtype)

    @pl.when(i == pl.num_programs(1) - 1)
    def _():
        hl_ref[...] = h
        nb_ref[...] = xext_ref[tt:tt + halo, :]


def _rglru_seq(xrg, buf8, h0, lw, layer, *, B, T, DR):
    tt = min(T, 256)
    nt = T // tt
    n_blocks, bw = lw["w_rg_a"].shape[1:3]

    def per_layer(shape):
        return pl.BlockSpec((None,) + shape, lambda b, i: (layer,) + (0,) * len(shape))

    per_seq8 = pl.BlockSpec((None, SUBLANES, DR), lambda b, i: (b, 0, 0))
    per_seq1 = pl.BlockSpec((None, 1, DR), lambda b, i: (b, 0, 0))
    body = functools.partial(_rglru_seq_body, tt=tt, n_blocks=n_blocks, bw=bw)
    return pl.pallas_call(
        body, grid=(B, nt),
        in_specs=[pl.BlockSpec((tt, DR), lambda b, i: (b * nt + i, 0)),
                  pl.BlockSpec((tt, DR), lambda b, i: (b * nt + i, 1)),
                  per_seq8, per_seq1,
                  per_layer((CONV_W, DR)), per_layer((1, DR)),
                  per_layer((n_blocks, bw, bw)), per_layer((n_blocks, bw, bw)),
                  per_layer((1, DR)), per_layer((1, DR)), per_layer((1, DR))],
        out_specs=[pl.BlockSpec((tt, DR), lambda b, i: (b * nt + i, 0)), per_seq1, per_seq8],
        out_shape=[jax.ShapeDtypeStruct((B * T, DR), BF16), jax.ShapeDtypeStruct((B, 1, DR), F32),
                   jax.ShapeDtypeStruct((B, SUBLANES, DR), F32)],
        scratch_shapes=[pltpu.VMEM((tt + SUBLANES, DR), F32), pltpu.VMEM((tt, DR), F32),
                        pltpu.VMEM((tt, DR), F32), pltpu.VMEM((tt, DR), F32), pltpu.VMEM((1, DR), F32)],
        compiler_params=_params(2), name="rglru_seq",
    )(xrg, xrg, buf8, h0, lw["conv_w"], lw["conv_b"], lw["w_rg_a"], lw["w_rg_x"],
      lw["b_rg_a"], lw["b_rg_x"], lw["rg_lambda"])


def _rglru_step_body(xr_ref, xg_ref, buf_ref, h0_ref, cw_ref, cb_ref, wa_ref, wx_ref, ba_ref, bx_ref, lam_ref,
                     y_ref, hl_ref, nb_ref, xc_ref, a_ref, u_ref, *, n_blocks, bw):
    xr = xr_ref[...]
    xc = cb_ref[...] + xr * cw_ref[CONV_W - 1:CONV_W, :]
    for j in range(CONV_W - 1):
        xc = xc + buf_ref[j] * cw_ref[j:j + 1, :]
    xc_ref[...] = xc
    _rglru_gates(xc_ref, a_ref, u_ref, wa_ref, wx_ref, ba_ref, bx_ref, lam_ref, n_blocks, bw)
    h = a_ref[...] * h0_ref[...] + u_ref[...]
    hl_ref[...] = h
    y_ref[...] = h * jax.nn.gelu(xg_ref[...])
    for j in range(CONV_W - 2):
        nb_ref[j] = buf_ref[j + 1]
    nb_ref[CONV_W - 2] = xr


def _rglru_step(xrg, bufT, h0, lw, layer, *, DB, DR):
    n_blocks, bw = lw["w_rg_a"].shape[1:3]

    def per_layer(shape):
        return pl.BlockSpec((None,) + shape, lambda i: (layer,) + (0,) * len(shape))

    tile = pl.BlockSpec((DB, DR), lambda i: (0, 0))
    body = functools.partial(_rglru_step_body, n_blocks=n_blocks, bw=bw)
    return pl.pallas_call(
        body, grid=(1,),
        in_specs=[tile, pl.BlockSpec((DB, DR), lambda i: (0, 1)),
                  per_layer((CONV_W - 1, DB, DR)), per_layer((DB, DR)),
                  per_layer((CONV_W, DR)), per_layer((1, DR)),
                  per_layer((n_blocks, bw, bw)), per_layer((n_blocks, bw, bw)),
                  per_layer((1, DR)), per_layer((1, DR)), per_layer((1, DR))],
        out_specs=[tile, tile, pl.BlockSpec((CONV_W - 1, DB, DR), lambda i: (0, 0, 0))],
        out_shape=[jax.ShapeDtypeStruct((DB, DR), F32), jax.ShapeDtypeStruct((DB, DR), F32),
                   jax.ShapeDtypeStruct((CONV_W - 1, DB, DR), F32)],
        scratch_shapes=[pltpu.VMEM((DB, DR), F32)] * 3,
        compiler_params=_params(1), name="rglru_step",
    )(xrg, xrg, bufT, h0, lw["conv_w"], lw["conv_b"], lw["w_rg_a"], lw["w_rg_x"],
      lw["b_rg_a"], lw["b_rg_x"], lw["rg_lambda"])


def _merge_body(ya_ref, yr_ref, wa_ref, wr_ref, ga_ref, gr_ref, ya2_ref, yr2_ref, ga2_ref, gr2_ref,
                o_ref, o2_ref, wab_ref, wrb_ref):
    first_row_tile = pl.program_id(1) == 0

    @pl.when(first_row_tile)
    def _():
        wab_ref[...] = wa_ref[...].astype(BF16)
        wrb_ref[...] = wr_ref[...].astype(BF16)

    def merged(ya, yr, ga, gr):
        za = jnp.dot(ya, wab_ref[...], preferred_element_type=F32)
        zr = jnp.dot(yr, wrb_ref[...], preferred_element_type=F32)
        return jax.nn.sigmoid(ga) * za + jax.nn.sigmoid(gr) * zr

    o_ref[...] = merged(ya_ref[...], yr_ref[...], ga_ref[...], gr_ref[...]).astype(o_ref.dtype)

    @pl.when(first_row_tile)
    def _():
        o2_ref[...] = merged(ya2_ref[...], yr2_ref[...], ga2_ref[...], gr2_ref[...]).astype(o2_ref.dtype)


def _merge(ya, yr, gates, ya2, yr2, gates2, w_up_att, w_up_rnn, layer):
    M, att = ya.shape
    m2 = ya2.shape[0]
    dr = yr.shape[1]
    D = w_up_att.shape[-1]
    tm = min(M, 512)
    tn = _pick_tile(D, 0)
    nj = D // tn
    return pl.pallas_call(
        _merge_body, grid=(nj, M // tm),
        in_specs=[pl.BlockSpec((tm, att), lambda j, i: (i, 0)),
                  pl.BlockSpec((tm, dr), lambda j, i: (i, 0)),
                  pl.BlockSpec((None, att, tn), lambda j, i: (layer, 0, j)),
                  pl.BlockSpec((None, dr, tn), lambda j, i: (layer, 0, j)),
                  pl.BlockSpec((tm, tn), lambda j, i: (i, j)),
                  pl.BlockSpec((tm, tn), lambda j, i: (i, nj + j)),
                  pl.BlockSpec((m2, att), lambda j, i: (0, 0)),
                  pl.BlockSpec((m2, dr), lambda j, i: (0, 0)),
                  pl.BlockSpec((m2, tn), lambda j, i: (0, j)),
                  pl.BlockSpec((m2, tn), lambda j, i: (0, nj + j))],
        out_specs=[pl.BlockSpec((tm, tn), lambda j, i: (i, j)), pl.BlockSpec((m2, tn), lambda j, i: (0, j))],
        out_shape=[jax.ShapeDtypeStruct((M, D), BF16), jax.ShapeDtypeStruct((m2, D), BF16)],
        scratch_shapes=[pltpu.VMEM((att, tn), BF16), pltpu.VMEM((dr, tn), BF16)],
        compiler_params=_params(2), name="merge",
    )(ya, yr, w_up_att, w_up_rnn, gates, gates, ya2, yr2, gates2, gates2)


def _silu_bf16(c):
    return (c * jax.nn.sigmoid(c)).astype(BF16)


def _in_projection(hp, hs, w_inT, layer, dims, o_xr):
    att, idw, dr, d = dims["att"], dims["idx_width"], dims["dr"], dims["d"]
    q_scale = dims["q_scale"]
    iq_scale = dims["idx_dim"] ** -0.5
    idx_dim = dims["idx_dim"]
    mm = functools.partial(_matmul, hp, w_inT, layer, a2=hs, w_t=True)
    q = mm(0, att, [(att, BF16)], lambda acc: [acc * q_scale], name="proj_q")
    kv = mm(att, 2 * att, [(2 * att, F32), (2 * att, BF16)], lambda acc: [acc, acc], name="proj_kv")
    iq = mm(3 * att, idw, [(idw, BF16)], lambda acc: [acc * iq_scale], name="proj_iq")
    ikw = mm(3 * att + idw, 2 * LANES, [(2 * LANES, F32), (idx_dim, BF16)],
             lambda acc: [acc, acc[:, :idx_dim]], tn=2 * LANES, name="proj_ikw")
    xrg = mm(o_xr, 2 * dr, [(2 * dr, F32)], lambda acc: [acc], name="proj_rg")
    gates = mm(o_xr + 2 * dr, 2 * d, [(2 * d, F32)], lambda acc: [acc], name="proj_gate")
    return tuple(tuple(q[g]) + tuple(kv[g]) + tuple(iq[g]) + tuple(ikw[g]) + tuple(xrg[g]) + tuple(gates[g])
                 for g in range(2))


def _finish_layer(xp, xs, att_p, att_s, st_p, st_s, nxt_p, nxt_s, lw, norm_g3, layer):
    d = xp.shape[1]
    merged_p, merged_s = _merge(*att_p, *att_s, lw["w_up_att"], lw["w_up_rnn"], layer)
    (yp,), (ys,) = _matmul(merged_p, lw["w_o"], layer, 0, d, [(d, F32)], lambda acc: [acc], a2=merged_s,
                           name="proj_o")
    mid = (4 * layer + 2, 3, 4)
    xp, hp = _postnorm(xp, yp, norm_g3, 4 * layer + 1, st_p, 2, nxt=(mid[0], st_p, mid[1], mid[2]))
    xs, hs = _postnorm(xs, ys, norm_g3, 4 * layer + 1, st_s, 2, nxt=(mid[0], st_s, mid[1], mid[2]))
    dff = lw["w_mlp_in"].shape[2]
    (hid_p,), (hid_s,) = _matmul(hp, lw["w_mlp_in"], layer, 0, dff, [(dff, BF16)],
                                 lambda acc: [jnp.square(jnp.maximum(acc, 0.0))], a2=hs, name="mlp_in")
    ff_p, ff_s = _matmul_ksplit(hid_p, hid_s, lw["w_mlp_out"], layer)
    first = 4 * (layer + 1)
    xp, hp = _postnorm(xp, ff_p, norm_g3, 4 * layer + 3, st_p, 5,
                       nxt=None if nxt_p is None else (first, nxt_p, 0, 1))
    xs, hs = _postnorm(xs, ff_s, norm_g3, 4 * layer + 3, st_s, 5,
                       nxt=None if nxt_s is None else (first, nxt_s, 0, 1))
    return xp, hp, xs, hs


def kernel(x_prompt, x_sample, cache_k, cache_v, cache_idx_k, state_rglru_h, state_conv, page_table, c_prompt, c_sample, w_ada, b_ada, norm_g, w_in, rel_bias, conv_w, conv_b, w_rg_a, b_rg_a, w_rg_x, b_rg_x, rg_lambda, w_up_att, w_up_rnn, w_o, w_mlp_in, w_mlp_out):
    B, T, D = x_prompt.shape
    DB, t_new, _ = x_sample.shape
    assert t_new == 1, "the sample group decodes one token per sequence"
    L = w_ada.shape[0]
    n_pool, page, n_heads, head_dim = cache_k.shape[1:]
    att = n_heads * head_dim
    idx_dim = cache_idx_k.shape[-1]
    DR = state_rglru_h.shape[-1]
    in_w = w_in.shape[-1]
    idx_heads = (in_w - 3 * att - idx_dim - 2 * DR - 2 * D) // (idx_dim + 1)
    idw = idx_heads * idx_dim
    n_pages = page_table.shape[1]
    past = n_pages * page
    assert idx_dim == LANES and idx_heads <= LANES and n_heads <= LANES and page == LANES
    dims = dict(att=att, idx_width=idw, dr=DR, d=D, idx_dim=idx_dim, q_scale=head_dim ** -0.5 * LOG2_E)
    o_ik = 3 * att + idw
    o_xr = o_ik + idx_dim + idx_heads
    assert o_ik % (2 * LANES) == 0 and o_xr + 2 * DR + 2 * D == in_w

    tq = min(T, 256)
    nq = T // tq
    topk_p = min(TOPK_MAX, T // 4)
    topk_s = min(TOPK_MAX, (past + 1) // 4)
    Mp = B * T
    Ms = 2 * SUBLANES
    assert DB <= Ms and B + Ms <= 4 * SUBLANES and tq % LANES == 0 and T % tq == 0 and tq >= MAX_DISTANCE

    c_all = jnp.zeros((4 * SUBLANES, D), F32).at[:B].set(c_prompt).at[B:B + DB].set(c_sample)
    b_ada3 = b_ada.reshape(L, 1, 6 * D)
    norm_g3 = norm_g.reshape(L * 4, 1, D)
    lw_all = dict(conv_w=conv_w, conv_b=conv_b.reshape(L, 1, DR), w_rg_a=w_rg_a, w_rg_x=w_rg_x,
                  b_rg_a=b_rg_a.reshape(L, 1, DR), b_rg_x=b_rg_x.reshape(L, 1, DR),
                  rg_lambda=rg_lambda.reshape(L, 1, DR), w_up_att=w_up_att, w_up_rnn=w_up_rnn, w_o=w_o,
                  w_mlp_in=w_mlp_in, w_mlp_out=w_mlp_out)

    bias_tiles = _bias_tiles(rel_bias, tq)
    rel_pad = jnp.pad(rel_bias, ((0, 0), (0, LANES - n_heads)))
    w_inT = jnp.swapaxes(w_in, 1, 2)
    conv_sT = jnp.swapaxes(state_conv, 1, 2)
    zero_buf8 = jnp.zeros((B, SUBLANES, DR), F32)
    zero_h = jnp.zeros((B, 1, DR), F32)

    xp = x_prompt.reshape(Mp, D)
    xs = jnp.pad(x_sample.reshape(DB, D), ((0, Ms - DB), (0, 0)))
    outs_p = [[] for _ in range(5)]
    outs_s = [[] for _ in range(5)]

    tn_ada = _pick_tile(6 * D, 0)
    streams_p, streams_s = [], []
    for l in range(L):
        (mod,) = _matmul(c_all, w_ada, l, 0, 6 * D, [(6 * D, F32)], lambda acc, b: [acc + b],
                         extras=[(b_ada3, pl.BlockSpec((None, 1, tn_ada), lambda j, i, l=l: (l, 0, j)))],
                         a_fn=_silu_bf16, name="ada")
        streams_p.append(_Stream(mod[:B].reshape(B, 1, 6 * D), T, Mp, D))
        streams_s.append(_Stream(mod[B:B + Ms], 1, Ms, D))
    streams_p.append(None)
    streams_s.append(None)

    hp = _prenorm(xp, norm_g3, 0, streams_p[0], 0, 1)
    hs = _prenorm(xs, norm_g3, 0, streams_s[0], 0, 1)
    for l in range(L):
        proj_p, proj_s = _in_projection(hp, hs, w_inT, l, dims, o_xr)

        q, kv, kvb, iq, ikw, ikb, xrg, gates_p = proj_p
        qT = jnp.swapaxes(q.reshape(B * nq, tq, att), 1, 2)
        iqT = jnp.swapaxes(iq.reshape(B * nq, tq, idw), 1, 2)
        vT = jnp.swapaxes(kvb[:, att:].reshape(B * nq, tq, att), 1, 2)
        wT = jnp.swapaxes(ikw[:, idx_dim:].reshape(B * nq, tq, LANES), 1, 2)
        yT = _dsa_prompt(qT, kvb, vT, iqT, ikb, wT, bias_tiles, B=B, T=T, tq=tq, topk=topk_p,
                         n_heads=n_heads, head_dim=head_dim, idx_heads=idx_heads, idx_dim=idx_dim)
        ya_p = jnp.swapaxes(yT, 1, 2).reshape(Mp, att)
        yr_p, h_last, nbuf = _rglru_seq(xrg, zero_buf8, zero_h, lw_all, l, B=B, T=T, DR=DR)
        outs_p[0].append(kv[:, :att].reshape(B, T, n_heads, head_dim))
        outs_p[1].append(kv[:, att:].reshape(B, T, n_heads, head_dim))
        outs_p[2].append(ikw[:, :idx_dim].reshape(B, T, idx_dim))
        outs_p[3].append(h_last.reshape(B, DR))
        outs_p[4].append(nbuf[:, SUBLANES - (CONV_W - 1):])

        q, kv, kvb, iq, ikw, ikb, xrg, gates_s = proj_s
        k_new = kv[:DB, :att]
        v_new = kv[:DB, att:]
        ik_new = ikw[:DB, :idx_dim]
        iq3 = iq[:DB].reshape(DB, idx_heads, idx_dim)
        w3 = jnp.broadcast_to(ikw[:DB, idx_dim:idx_dim + idx_heads, None], (DB, idx_heads, page))
        new_pages = jnp.zeros((DB, page, idx_dim), F32).at[:, 0].set(ik_new)
        scores = _sample_scores(page_table, iq3, w3, cache_idx_k, new_pages, l)
        idx = _sample_topk(scores.reshape(DB, -1), topk_s)
        q_f32 = q[:DB].astype(F32) / dims["q_scale"]
        ya_s = _sample_attend(idx, page_table, q_f32.reshape(DB, n_heads, head_dim),
                              k_new.reshape(DB, n_heads, head_dim), v_new.reshape(DB, n_heads, head_dim),
                              rel_pad, cache_k, cache_v, l)
        ya_s = jnp.pad(ya_s.reshape(DB, att), ((0, Ms - DB), (0, 0))).astype(BF16)
        y_s, h_new, nbufT = _rglru_step(xrg, conv_sT, state_rglru_h, lw_all, l, DB=DB, DR=DR)
        yr_s = jnp.pad(y_s, ((0, Ms - DB), (0, 0))).astype(BF16)
        outs_s[0].append(k_new.reshape(DB, 1, n_heads, head_dim))
        outs_s[1].append(v_new.reshape(DB, 1, n_heads, head_dim))
        outs_s[2].append(ik_new.reshape(DB, 1, idx_dim))
        outs_s[3].append(h_new)
        outs_s[4].append(jnp.swapaxes(nbufT, 0, 1))

        xp, hp, xs, hs = _finish_layer(xp, xs, (ya_p, yr_p, gates_p), (ya_s, yr_s, gates_s),
                                       streams_p[l], streams_s[l], streams_p[l + 1], streams_s[l + 1],
                                       lw_all, norm_g3, l)

    y_prompt = xp.reshape(B, T, D)
    y_sample = xs[:DB].reshape(DB, 1, D)
    return (y_prompt, y_sample, *[jnp.stack(o) for o in outs_p], *[jnp.stack(o) for o in outs_s])
```

```python
import functools
import math

import jax
import jax.numpy as jnp
from jax import lax
from jax.experimental import pallas as pl
from jax.experimental.pallas import tpu as pltpu

F32 = jnp.float32
BF16 = jnp.bfloat16
I32 = jnp.int32

V7X_VMEM_BYTES = 64 * 1024 * 1024
VMEM_LIMIT = V7X_VMEM_BYTES - 8 * 1024 * 1024
LANES = 128
SUBLANES = 8

TOPK_MAX = 256
MAX_DISTANCE = 128
RG_C = 8.0
NORM_EPS = 1e-6
CONV_W = 4
MASK_NEG = -1e30
LOG2_E = math.log2(math.e)
ATTN_HEAD_GROUP = 4
INT_MIN = -2 ** 31
INT_MAX = 2 ** 31 - 1


def _params(n_axes):
    return pltpu.CompilerParams(dimension_semantics=("arbitrary",) * n_axes, vmem_limit_bytes=VMEM_LIMIT)


def _pick_tile(n, col0, candidates=(512, 256, 128)):
    for t in candidates:
        if n % t == 0 and col0 % t == 0:
            return t
    raise ValueError(f"no lane tile for width {n} at column {col0}")


def _mm_body(*refs, n_extra, n_out, cast_w, w_t, has2, a_fn, epilogue):
    a_ref, w_ref = refs[0], refs[1]
    pos = 2
    a2_ref = None
    if has2:
        a2_ref = refs[pos]
        pos += 1
    extra = refs[pos:pos + n_extra]
    pos += n_extra
    outs = refs[pos:pos + n_out]
    pos += n_out
    outs2 = refs[pos:pos + n_out] if has2 else ()
    pos += len(outs2)
    first_row_tile = pl.program_id(1) == 0
    if cast_w:
        wb_ref = refs[pos]

        @pl.when(first_row_tile)
        def _():
            wb_ref[...] = (w_ref[0] if w_t else w_ref[...]).astype(BF16)

        w_ref = wb_ref

    def project(a):
        if a_fn is not None:
            a = a_fn(a)
        contract = (((1,), (1,)), ((), ())) if w_t else (((1,), (0,)), ((), ()))
        acc = lax.dot_general(a, w_ref[...], contract, preferred_element_type=F32)
        return epilogue(acc, *[e[...] for e in extra])

    for o_ref, r in zip(outs, project(a_ref[...])):
        o_ref[...] = r.astype(o_ref.dtype)
    if has2:
        @pl.when(first_row_tile)
        def _():
            for o_ref, r in zip(outs2, project(a2_ref[...])):
                o_ref[...] = r.astype(o_ref.dtype)


def _matmul(a, w, layer, col0, n, outs, epilogue, *, a2=None, extras=(), a_fn=None, tn=None, w_t=False,
            name="mm"):
    M, K = a.shape
    tm = min(M, 1024)
    if tn is None:
        tn = _pick_tile(n, 0 if w_t else col0)
    assert M % tm == 0 and n % tn == 0
    cast_w = w.dtype != BF16
    if w_t:
        assert col0 % SUBLANES == 0 and cast_w
        w_spec = pl.BlockSpec((pl.Element(1), pl.Element(tn), pl.Element(K)),
                              lambda j, i: (layer, pl.multiple_of(col0 + j * tn, SUBLANES), 0))
    else:
        assert col0 % tn == 0
        w_spec = pl.BlockSpec((None, K, tn), lambda j, i: (layer, 0, col0 // tn + j))
    in_specs = [pl.BlockSpec((tm, K), lambda j, i: (i, 0)), w_spec]
    args = [a, w]
    out_specs = [pl.BlockSpec((tm, tn * wd // n), lambda j, i: (i, j)) for wd, _ in outs]
    out_shape = [jax.ShapeDtypeStruct((M, wd), dt) for wd, dt in outs]
    if a2 is not None:
        m2 = a2.shape[0]
        in_specs.append(pl.BlockSpec((m2, K), lambda j, i: (0, 0)))
        args.append(a2)
        out_specs += [pl.BlockSpec((m2, tn * wd // n), lambda j, i: (0, j)) for wd, _ in outs]
        out_shape += [jax.ShapeDtypeStruct((m2, wd), dt) for wd, dt in outs]
    in_specs += [spec for _, spec in extras]
    args += [arr for arr, _ in extras]
    body = functools.partial(_mm_body, n_extra=len(extras), n_out=len(outs), cast_w=cast_w, w_t=w_t,
                             has2=a2 is not None, a_fn=a_fn, epilogue=epilogue)
    res = pl.pallas_call(
        body, grid=(n // tn, M // tm), in_specs=in_specs, out_specs=out_specs, out_shape=out_shape,
        scratch_shapes=[pltpu.VMEM((tn, K) if w_t else (K, tn), BF16)] if cast_w else [],
        compiler_params=_params(2), name=name,
    )(*args)
    return res if a2 is None else (res[:len(outs)], res[len(outs):])


def _mm_ksplit_body(a_ref, w_ref, a2_ref, o_ref, o2_ref, wb_ref, *, tm):
    k = pl.program_id(2)
    i = pl.program_id(3)

    @pl.when(i == 0)
    def _():
        wb_ref[...] = w_ref[...].astype(BF16)

    def accumulate(dst, rows, a_ref):
        @pl.when(k == 0)
        def _():
            dst[rows, :] = jnp.dot(a_ref[...], wb_ref[...], preferred_element_type=F32)

        @pl.when(k > 0)
        def _():
            dst[rows, :] += jnp.dot(a_ref[...], wb_ref[...], preferred_element_type=F32)

    accumulate(o_ref, pl.ds(pl.multiple_of(i * tm, tm), tm), a_ref)

    @pl.when(i == 0)
    def _():
        accumulate(o2_ref, slice(None), a2_ref)


def _matmul_ksplit(a, a2, w, layer):
    M, K = a.shape
    m2 = a2.shape[0]
    N = w.shape[2]
    tm = min(M, 1024)
    tk = min(K, 4096)
    tn = _pick_tile(N, 0)
    n_panels = max(1, (M * tn * 4) // (8 * 1024 * 1024))
    panel = M // n_panels
    assert M % tm == 0 and K % tk == 0 and panel % tm == 0
    ni = panel // tm
    res = pl.pallas_call(
        functools.partial(_mm_ksplit_body, tm=tm), grid=(n_panels, N // tn, K // tk, ni),
        in_specs=[pl.BlockSpec((tm, tk), lambda p, j, k, i: (p * ni + i, k)),
                  pl.BlockSpec((None, tk, tn), lambda p, j, k, i: (layer, k, j)),
                  pl.BlockSpec((m2, tk), lambda p, j, k, i: (0, k))],
        out_specs=[pl.BlockSpec((panel, tn), lambda p, j, k, i: (p, j)),
                   pl.BlockSpec((m2, tn), lambda p, j, k, i: (p, j))],
        out_shape=[jax.ShapeDtypeStruct((M, N), F32), jax.ShapeDtypeStruct((n_panels * m2, N), F32)],
        scratch_shapes=[pltpu.VMEM((tk, tn), BF16)],
        compiler_params=_params(4), name="mm_ksplit",
    )(a, w, a2)
    return res[0], res[1][:m2]


def _rms(x, g):
    return x * lax.rsqrt(jnp.mean(x * x, axis=-1, keepdims=True) + NORM_EPS) * g


def _prenorm_body(x_ref, g_ref, sh_ref, sc_ref, h_ref):
    h = _rms(x_ref[...], g_ref[...]) * (1.0 + sc_ref[...]) + sh_ref[...]
    h_ref[...] = h.astype(h_ref.dtype)


def _postnorm_body(*refs, with_next):
    if with_next:
        x_ref, y_ref, gpost_ref, gate_ref, gpre_ref, sh_ref, sc_ref, xo_ref, h_ref = refs
    else:
        x_ref, y_ref, gpost_ref, gate_ref, xo_ref = refs
    x = x_ref[...] + gate_ref[...] * _rms(y_ref[...], gpost_ref[...])
    xo_ref[...] = x
    if with_next:
        h = _rms(x, gpre_ref[...]) * (1.0 + sc_ref[...]) + sh_ref[...]
        h_ref[...] = h.astype(h_ref.dtype)


class _Stream:
    def __init__(self, mod, rows_per_seq, n_rows, d):
        self.mod = mod
        self.rows_per_seq = rows_per_seq
        self.tm = min(256, n_rows)
        self.d = d

    def mod_spec(self, chunk):
        tm, d = self.tm, self.d
        if self.rows_per_seq > 1:
            rps = self.rows_per_seq
            return pl.BlockSpec((None, 1, d), lambda i: (i * tm // rps, 0, chunk))
        return pl.BlockSpec((tm, d), lambda i: (i, chunk))


def _norm_spec(g_index, d):
    return pl.BlockSpec((None, 1, d), lambda i: (g_index, 0, 0))


def _prenorm(x, norm_g3, g_index, st, shift_chunk, scale_chunk):
    M, D = x.shape
    tm = st.tm
    row = pl.BlockSpec((tm, D), lambda i: (i, 0))
    return pl.pallas_call(
        _prenorm_body, grid=(M // tm,),
        in_specs=[row, _norm_spec(g_index, D), st.mod_spec(shift_chunk), st.mod_spec(scale_chunk)],
        out_specs=row, out_shape=jax.ShapeDtypeStruct((M, D), BF16),
        compiler_params=_params(1), name="prenorm",
    )(x, norm_g3, st.mod, st.mod)


def _postnorm(x, y, norm_g3, g_post, st, gate_chunk, nxt=None):
    M, D = x.shape
    tm = st.tm
    row = pl.BlockSpec((tm, D), lambda i: (i, 0))
    in_specs = [row, row, _norm_spec(g_post, D), st.mod_spec(gate_chunk)]
    args = [x, y, norm_g3, st.mod]
    out_specs = [row]
    out_shape = [jax.ShapeDtypeStruct((M, D), F32)]
    if nxt is not None:
        g_pre, st_n, sh_c, sc_c = nxt
        in_specs += [_norm_spec(g_pre, D), st_n.mod_spec(sh_c), st_n.mod_spec(sc_c)]
        args += [norm_g3, st_n.mod, st_n.mod]
        out_specs.append(row)
        out_shape.append(jax.ShapeDtypeStruct((M, D), BF16))
    res = pl.pallas_call(
        functools.partial(_postnorm_body, with_next=nxt is not None), grid=(M // tm,),
        in_specs=in_specs, out_specs=out_specs, out_shape=out_shape,
        compiler_params=_params(1), name="postnorm",
    )(*args)
    return res if nxt is not None else (res[0], None)


def _t5_bucket(dist, n_buckets):
    max_exact = n_buckets // 2
    d = jnp.maximum(dist, 0)
    d_f = jnp.maximum(d, 1).astype(F32)
    large = max_exact + (jnp.log(d_f / max_exact) / math.log(MAX_DISTANCE / max_exact)
                         * (n_buckets - max_exact)).astype(I32)
    large = jnp.minimum(large, n_buckets - 1)
    return jnp.where(d < max_exact, d, large)


def _bias_tiles_body(rel_ref, o_ref, *, tq, n_buckets):
    which = pl.program_id(0)
    h = pl.program_id(1)
    r = lax.broadcasted_iota(I32, (tq, tq), 0)
    c = lax.broadcasted_iota(I32, (tq, tq), 1)
    bucket = _t5_bucket(which * tq + c - r, n_buckets)
    far = rel_ref[n_buckets - 1, h]
    acc = jnp.zeros((tq, tq), F32)
    for n in range(n_buckets - 1):
        acc = jnp.where(bucket == n, rel_ref[n, h] - far, acc)
    o_ref[...] = acc * LOG2_E


def _bias_tiles(rel_bias, tq):
    nb, n_heads = rel_bias.shape
    return pl.pallas_call(
        functools.partial(_bias_tiles_body, tq=tq, n_buckets=nb), grid=(2, n_heads),
        in_specs=[pl.BlockSpec(memory_space=pltpu.SMEM)],
        out_specs=pl.BlockSpec((None, None, tq, tq), lambda w, h: (w, h, 0, 0)),
        out_shape=jax.ShapeDtypeStruct((2, n_heads, tq, tq), F32),
        compiler_params=_params(2), name="bias_tiles",
    )(rel_bias)


def _attn_body(qT_ref, k_ref, vT_ref, iqT_ref, ik_ref, wT_ref, bias_ref, o_ref,
               key_ref, mask_ref, p_ref, m_ref, l_ref, acc_ref, lg_ref,
               *, topk, n_heads, head_dim, idx_heads, idx_dim, tq, n_chunks):
    i = pl.program_id(1)
    nck = i + 1
    w_scale = idx_heads ** -0.5
    q_pos = i * tq + lax.broadcasted_iota(I32, (tq, tq), 1)
    r_loc = lax.broadcasted_iota(I32, (tq, tq), 0)

    def rows(c):
        return pl.ds(pl.multiple_of(c * tq, tq), tq)

    def score_chunk(c, carry):
        ik_c = ik_ref[rows(c), :]
        sc = None
        for h in range(idx_heads):
            s = jnp.dot(ik_c, iqT_ref[h * idx_dim:(h + 1) * idx_dim, :], preferred_element_type=F32)
            contrib = jnp.maximum(s, 0.0) * (wT_ref[h:h + 1, :] * w_scale)
            sc = contrib if sc is None else sc + contrib
        bits = pltpu.bitcast(sc + 0.0, I32)
        key = jnp.where(bits < 0, bits ^ INT_MAX, bits)
        key_ref[rows(c), :] = jnp.where(c * tq + r_loc <= q_pos, key, INT_MIN)
        return carry

    lax.fori_loop(0, nck, score_chunk, 0)

    def count(pred):
        def body(c, cnt):
            hit = pred(key_ref[rows(c), :], c).astype(I32).reshape(tq // SUBLANES, SUBLANES, tq)
            return cnt + jnp.sum(hit, axis=0)
        cnt = lax.fori_loop(0, nck, body, jnp.zeros((SUBLANES, tq), I32))
        return jnp.sum(cnt, axis=0, keepdims=True)

    thr = jnp.where(count(lambda k, c: k >= 0) >= topk, 0, INT_MIN).astype(I32)

    def thr_bit(s, thr):
        cand = thr | jnp.left_shift(jnp.int32(1), 30 - s)
        return jnp.where(count(lambda k, c: k >= cand) >= topk, cand, thr)

    thr = lax.fori_loop(0, 31, thr_bit, thr)

    n_gt = count(lambda k, c: k > thr)
    n_ge = count(lambda k, c: k >= thr)
    need = topk - n_gt
    p_ref[...] = jnp.full((1, tq), INT_MAX, I32)
    has_ties = jnp.max(jnp.where((n_ge > topk) & (thr > INT_MIN), 1, 0)) > 0

    @pl.when(has_ties)
    def _():
        n_bits = max(1, (n_chunks * tq - 1).bit_length())

        def pos_bit(s, p):
            cand = p | jnp.left_shift(jnp.int32(1), n_bits - 1 - s)
            below = count(lambda k, c: (k == thr) & (c * tq + r_loc < cand))
            return jnp.where(below < need, cand, p)

        p_ref[...] = lax.fori_loop(0, n_bits, pos_bit, jnp.zeros((1, tq), I32))

    p_last = p_ref[...]

    def mask_chunk(c, carry):
        k = key_ref[rows(c), :]
        pos = c * tq + r_loc
        keep = ((k > thr) | ((k == thr) & (pos <= p_last))) & (pos <= q_pos)
        mask_ref[c] = jnp.where(keep, 0.0, MASK_NEG)
        return carry

    lax.fori_loop(0, nck, mask_chunk, 0)

    m_ref[...] = jnp.full(m_ref.shape, MASK_NEG, F32)
    l_ref[...] = jnp.zeros(l_ref.shape, F32)
    acc_ref[...] = jnp.zeros(acc_ref.shape, F32)
    group = lg_ref.shape[0]

    def logits(c, carry, h0, near):
        mask_c = mask_ref[c]
        for g in range(group):
            h = h0 + g
            hs = slice(h * head_dim, (h + 1) * head_dim)
            s = jnp.dot(k_ref[rows(c), hs], qT_ref[hs, :], preferred_element_type=F32) + mask_c
            if near:
                s = s + bias_ref[i - c, h]
            lg_ref[g, rows(c), :] = s
            m_ref[h] = jnp.maximum(m_ref[h], jnp.max(s, axis=0, keepdims=True))
        return carry

    def values(c, carry, h0):
        for g in range(group):
            h = h0 + g
            hs = slice(h * head_dim, (h + 1) * head_dim)
            p = jnp.exp2(lg_ref[g, rows(c), :] - m_ref[h])
            l_ref[h] += jnp.sum(p, axis=0, keepdims=True)
            acc_ref[h] += jnp.dot(vT_ref[c, hs, :], p.astype(BF16), preferred_element_type=F32)
        return carry

    c_near = jnp.maximum(i - 1, 0)
    for h0 in range(0, n_heads, group):
        lax.fori_loop(0, c_near, functools.partial(logits, h0=h0, near=False), 0)
        lax.fori_loop(c_near, nck, functools.partial(logits, h0=h0, near=True), 0)
        lax.fori_loop(0, nck, functools.partial(values, h0=h0), 0)
    for h in range(n_heads):
        hs = slice(h * head_dim, (h + 1) * head_dim)
        o_ref[hs, :] = (acc_ref[h] / l_ref[h]).astype(o_ref.dtype)


def _dsa_prompt(qT, kb, vT, iqT, ikb, wT, bias, *, B, T, tq, topk, n_heads, head_dim, idx_heads, idx_dim):
    nq = T // tq
    att = n_heads * head_dim
    once = dict(pipeline_mode=pl.Buffered(1))
    body = functools.partial(_attn_body, topk=topk, n_heads=n_heads, head_dim=head_dim,
                             idx_heads=idx_heads, idx_dim=idx_dim, tq=tq, n_chunks=nq)
    return pl.pallas_call(
        body, grid=(B, nq),
        in_specs=[
            pl.BlockSpec((None, att, tq), lambda b, i: (b * nq + i, 0, 0)),
            pl.BlockSpec((T, att), lambda b, i: (b, 0), **once),
            pl.BlockSpec((nq, att, tq), lambda b, i: (b, 0, 0), **once),
            pl.BlockSpec((None, idx_heads * idx_dim, tq), lambda b, i: (b * nq + i, 0, 0)),
            pl.BlockSpec((T, idx_dim), lambda b, i: (b, 0), **once),
            pl.BlockSpec((None, wT.shape[1], tq), lambda b, i: (b * nq + i, 0, 0)),
            pl.BlockSpec((2, n_heads, tq, tq), lambda b, i: (0, 0, 0, 0), **once),
        ],
        out_specs=pl.BlockSpec((None, att, tq), lambda b, i: (b * nq + i, 0, 0)),
        out_shape=jax.ShapeDtypeStruct((B * nq, att, tq), BF16),
        scratch_shapes=[pltpu.VMEM((T, tq), I32), pltpu.VMEM((nq, tq, tq), F32), pltpu.VMEM((1, tq), I32),
                        pltpu.VMEM((n_heads, 1, tq), F32), pltpu.VMEM((n_heads, 1, tq), F32),
                        pltpu.VMEM((n_heads, head_dim, tq), F32),
                        pltpu.VMEM((math.gcd(ATTN_HEAD_GROUP, n_heads), T, tq), F32)],
        compiler_params=_params(2), name="dsa_prompt",
    )(qT, kb, vT, iqT, ikb, wT, bias)


def _sample_scores_body(pt_ref, iq_ref, w_ref, *refs, n_groups, idx_heads):
    page_refs, new_ref, o_ref = refs[:SUBLANES], refs[SUBLANES], refs[SUBLANES + 1]
    j = pl.program_id(1)

    def page_scores(keys):
        s = lax.dot_general(iq_ref[...], keys.astype(BF16), (((1,), (1,)), ((), ())),
                            preferred_element_type=F32)
        w = w_ref[...] * idx_heads ** -0.5
        return jnp.sum(jnp.maximum(s, 0.0) * w, axis=0, keepdims=True) + 0.0

    group = pl.ds(pl.multiple_of(j * SUBLANES, SUBLANES), SUBLANES)

    @pl.when(j < n_groups)
    def _():
        o_ref[group, :] = jnp.concatenate([page_scores(r[...]) for r in page_refs], axis=0)

    @pl.when(j == n_groups)
    def _():
        sc = jnp.broadcast_to(page_scores(new_ref[...]), (SUBLANES, new_ref.shape[0]))
        lane = lax.broadcasted_iota(I32, sc.shape, 1)
        row = lax.broadcasted_iota(I32, sc.shape, 0)
        o_ref[group, :] = jnp.where((lane == 0) & (row == 0), sc, -jnp.inf)


def _sample_scores(page_table, iq3, w3, cache_idx_k, new_pages, layer):
    DB, n_pages = page_table.shape
    _, idx_heads, idx_dim = iq3.shape
    page = cache_idx_k.shape[2]
    assert n_pages % SUBLANES == 0
    n_groups = n_pages // SUBLANES
    rows = n_pages + SUBLANES

    def page_spec(r):
        return pl.BlockSpec(
            (None, None, page, idx_dim),
            lambda b, j, pt: (layer, pt[b, jnp.minimum(j, n_groups - 1) * SUBLANES + r], 0, 0))

    grid_spec = pltpu.PrefetchScalarGridSpec(
        num_scalar_prefetch=1, grid=(DB, n_groups + 1),
        in_specs=[pl.BlockSpec((None, idx_heads, idx_dim), lambda b, j, pt: (b, 0, 0)),
                  pl.BlockSpec((None, idx_heads, page), lambda b, j, pt: (b, 0, 0))]
        + [page_spec(r) for r in range(SUBLANES)]
        + [pl.BlockSpec((None, page, idx_dim), lambda b, j, pt: (b, 0, 0))],
        out_specs=pl.BlockSpec((None, rows, page), lambda b, j, pt: (b, 0, 0)),
    )
    return pl.pallas_call(
        functools.partial(_sample_scores_body, n_groups=n_groups, idx_heads=idx_heads),
        grid_spec=grid_spec, out_shape=jax.ShapeDtypeStruct((DB, rows, page), F32),
        compiler_params=_params(2), name="sample_scores",
    )(page_table, iq3, w3, *([cache_idx_k] * SUBLANES), new_pages)


def _topk_body(s_ref, o_ref, work_ref, *, topk):
    work_ref[...] = s_ref[...]
    lane = lax.broadcasted_iota(I32, s_ref.shape, 1)
    slot = lax.broadcasted_iota(I32, o_ref.shape, 1)

    def body(k, acc):
        s = work_ref[...]
        m = jnp.max(s, axis=1, keepdims=True)
        idx = jnp.min(jnp.where(s == m, lane, INT_MAX), axis=1, keepdims=True)
        work_ref[...] = jnp.where(lane == idx, -jnp.inf, s)
        return jnp.where(slot == k, idx, acc)

    o_ref[...] = lax.fori_loop(0, topk, body, jnp.zeros(o_ref.shape, I32))


def _sample_topk(scores2d, topk):
    DB, n = scores2d.shape
    return pl.pallas_call(
        functools.partial(_topk_body, topk=topk),
        out_shape=jax.ShapeDtypeStruct((DB, topk), I32),
        scratch_shapes=[pltpu.VMEM((DB, n), F32)],
        compiler_params=pltpu.CompilerParams(vmem_limit_bytes=VMEM_LIMIT), name="sample_topk",
    )(scores2d)


def _sample_attend_body(idx_sm, pt_sm, q_ref, knew_ref, vnew_ref, idxc_ref, rel_ref, ck_hbm, cv_hbm, o_ref,
                        kbuf, vbuf, sem, *, layer, past, page, topk, n_heads, head_dim, n_buckets):
    b = pl.program_id(0)

    def row_copies(k):
        p = jnp.minimum(idx_sm[b, k], past - 1)
        phys = pt_sm[b, p // page]
        off = p % page
        return (pltpu.make_async_copy(ck_hbm.at[layer, phys, off], kbuf.at[:, k, :], sem.at[0]),
                pltpu.make_async_copy(cv_hbm.at[layer, phys, off], vbuf.at[:, k, :], sem.at[1]))

    def start(k, carry):
        for cp in row_copies(k):
            cp.start()
        return carry

    def wait(k, carry):
        for cp in row_copies(k):
            cp.wait()
        return carry

    lax.fori_loop(0, topk, start, 0)
    lax.fori_loop(0, topk, wait, 0)

    idxc = idxc_ref[...]
    is_new = idxc >= past
    bucket = _t5_bucket(past - idxc, n_buckets)
    bias = jnp.zeros((topk, LANES), F32)
    for n in range(n_buckets):
        bias = jnp.where(bucket == n, rel_ref[n:n + 1, :], bias)
    for h in range(n_heads):
        row_h = slice(h, h + 1)
        ks = jnp.where(is_new, knew_ref[row_h, :], kbuf[h])
        vs = jnp.where(is_new, vnew_ref[row_h, :], vbuf[h])
        s = jnp.sum(ks * q_ref[row_h, :], axis=1, keepdims=True) * head_dim ** -0.5 + bias[:, h:h + 1]
        e = jnp.exp(s - jnp.max(s, axis=0, keepdims=True))
        den = jnp.sum(e, axis=0, keepdims=True)
        o_ref[row_h, :] = jnp.sum(e * vs, axis=0, keepdims=True) / den


def _sample_attend(idx, page_table, q3, knew3, vnew3, rel_pad, cache_k, cache_v, layer):
    DB, topk = idx.shape
    page, n_heads, head_dim = cache_k.shape[2:]
    past = page_table.shape[1] * page
    row = pl.BlockSpec((None, n_heads, head_dim), lambda b, ix, pt: (b, 0, 0))
    grid_spec = pltpu.PrefetchScalarGridSpec(
        num_scalar_prefetch=2, grid=(DB,),
        in_specs=[row, row, row,
                  pl.BlockSpec((None, topk, 1), lambda b, ix, pt: (b, 0, 0)),
                  pl.BlockSpec(rel_pad.shape, lambda b, ix, pt: (0, 0)),
                  pl.BlockSpec(memory_space=pl.ANY), pl.BlockSpec(memory_space=pl.ANY)],
        out_specs=row,
        scratch_shapes=[pltpu.VMEM((n_heads, topk, head_dim), F32), pltpu.VMEM((n_heads, topk, head_dim), F32),
                        pltpu.SemaphoreType.DMA((2,))],
    )
    body = functools.partial(_sample_attend_body, layer=layer, past=past, page=page, topk=topk,
                             n_heads=n_heads, head_dim=head_dim, n_buckets=rel_pad.shape[0])
    return pl.pallas_call(
        body, grid_spec=grid_spec, out_shape=jax.ShapeDtypeStruct((DB, n_heads, head_dim), F32),
        compiler_params=_params(1), name="sample_attend",
    )(idx, page_table, q3, knew3, vnew3, idx.reshape(DB, topk, 1), rel_pad, cache_k, cache_v)


def _softplus(z):
    return jnp.maximum(z, 0.0) + jnp.log1p(jnp.exp(-jnp.abs(z)))


def _one_minus_exp(x):
    e = jnp.exp(x)
    log_e = jnp.log(e)
    return jnp.where(e == 1.0, -x, (1.0 - e) * x / jnp.where(log_e == 0.0, 1.0, log_e))


def _rglru_gates(xc_ref, a_ref, u_ref, wa_ref, wx_ref, ba_ref, bx_ref, lam_ref, n_blocks, bw):
    for n in range(n_blocks):
        cs = slice(n * bw, (n + 1) * bw)
        xb = xc_ref[:, cs]
        xb16 = xb.astype(BF16)
        r = jax.nn.sigmoid(jnp.dot(xb16, wa_ref[n].astype(BF16), preferred_element_type=F32) + ba_ref[:, cs])
        ig = jax.nn.sigmoid(jnp.dot(xb16, wx_ref[n].astype(BF16), preferred_element_type=F32) + bx_ref[:, cs])
        log_a = -RG_C * r * _softplus(-lam_ref[:, cs])
        a_ref[:, cs] = jnp.exp(log_a)
        u_ref[:, cs] = jnp.sqrt(_one_minus_exp(2.0 * log_a)) * ig * xb


def _rglru_seq_body(xr_ref, xg_ref, buf_ref, h0_ref, cw_ref, cb_ref, wa_ref, wx_ref, ba_ref, bx_ref, lam_ref,
                    y_ref, hl_ref, nb_ref, xext_ref, xc_ref, a_ref, u_ref, hc_ref, *, tt, n_blocks, bw):
    i = pl.program_id(1)
    halo = SUBLANES

    @pl.when(i == 0)
    def _():
        xext_ref[0:halo, :] = buf_ref[...]
        hc_ref[...] = h0_ref[...]

    @pl.when(i > 0)
    def _():
        xext_ref[0:halo, :] = xext_ref[tt:tt + halo, :]

    xext_ref[halo:halo + tt, :] = xr_ref[...]
    xc = cb_ref[...] + xext_ref[halo - 3:halo - 3 + tt, :] * cw_ref[0:1, :]
    for j in range(1, CONV_W):
        xc = xc + xext_ref[halo - 3 + j:halo - 3 + j + tt, :] * cw_ref[j:j + 1, :]
    xc_ref[...] = xc
    _rglru_gates(xc_ref, a_ref, u_ref, wa_ref, wx_ref, ba_ref, bx_ref, lam_ref, n_blocks, bw)

    def step(t, h):
        row = pl.ds(t, 1)
        h = a_ref[row, :] * h + u_ref[row, :]
        u_ref[row, :] = h
        return h

    h = lax.fori_loop(0, tt, step, hc_ref[...], unroll=8)
    hc_ref[...] = h
    y_ref[...] = (u_ref[...] * jax.nn.gelu(xg_ref[...])).astype(y_ref.dtype)

    @pl.when(i == pl.num_programs(1) - 1)
    def _():
        hl_ref[...] = h
        nb_ref[...] = xext_ref[tt:tt + halo, :]


def _rglru_seq(xrg, buf8, h0, lw, layer, *, B, T, DR):
    tt = min(T, 256)
    nt = T // tt
    n_blocks, bw = lw["w_rg_a"].shape[1:3]

    def per_layer(shape):
        return pl.BlockSpec((None,) + shape, lambda b, i: (layer,) + (0,) * len(shape))

    per_seq8 = pl.BlockSpec((None, SUBLANES, DR), lambda b, i: (b, 0, 0))
    per_seq1 = pl.BlockSpec((None, 1, DR), lambda b, i: (b, 0, 0))
    body = functools.partial(_rglru_seq_body, tt=tt, n_blocks=n_blocks, bw=bw)
    return pl.pallas_call(
        body, grid=(B, nt),
        in_specs=[pl.BlockSpec((tt, DR), lambda b, i: (b * nt + i, 0)),
                  pl.BlockSpec((tt, DR), lambda b, i: (b * nt + i, 1)),
                  per_seq8, per_seq1,
                  per_layer((CONV_W, DR)), per_layer((1, DR)),
                  per_layer((n_blocks, bw, bw)), per_layer((n_blocks, bw, bw)),
                  per_layer((1, DR)), per_layer((1, DR)), per_layer((1, DR))],
        out_specs=[pl.BlockSpec((tt, DR), lambda b, i: (b * nt + i, 0)), per_seq1, per_seq8],
        out_shape=[jax.ShapeDtypeStruct((B * T, DR), BF16), jax.ShapeDtypeStruct((B, 1, DR), F32),
                   jax.ShapeDtypeStruct((B, SUBLANES, DR), F32)],
        scratch_shapes=[pltpu.VMEM((tt + SUBLANES, DR), F32), pltpu.VMEM((tt, DR), F32),
                        pltpu.VMEM((tt, DR), F32), pltpu.VMEM((tt, DR), F32), pltpu.VMEM((1, DR), F32)],
        compiler_params=_params(2), name="rglru_seq",
    )(xrg, xrg, buf8, h0, lw["conv_w"], lw["conv_b"], lw["w_rg_a"], lw["w_rg_x"],
      lw["b_rg_a"], lw["b_rg_x"], lw["rg_lambda"])


def _rglru_step_body(xr_ref, xg_ref, buf_ref, h0_ref, cw_ref, cb_ref, wa_ref, wx_ref, ba_ref, bx_ref, lam_ref,
                     y_ref, hl_ref, nb_ref, xc_ref, a_ref, u_ref, *, n_blocks, bw):
    xr = xr_ref[...]
    xc = cb_ref[...] + xr * cw_ref[CONV_W - 1:CONV_W, :]
    for j in range(CONV_W - 1):
        xc = xc + buf_ref[j] * cw_ref[j:j + 1, :]
    xc_ref[...] = xc
    _rglru_gates(xc_ref, a_ref, u_ref, wa_ref, wx_ref, ba_ref, bx_ref, lam_ref, n_blocks, bw)
    h = a_ref[...] * h0_ref[...] + u_ref[...]
    hl_ref[...] = h
    y_ref[...] = h * jax.nn.gelu(xg_ref[...])
    for j in range(CONV_W - 2):
        nb_ref[j] = buf_ref[j + 1]
    nb_ref[CONV_W - 2] = xr


def _rglru_step(xrg, bufT, h0, lw, layer, *, DB, DR):
    n_blocks, bw = lw["w_rg_a"].shape[1:3]

    def per_layer(shape):
        return pl.BlockSpec((None,) + shape, lambda i: (layer,) + (0,) * len(shape))

    tile = pl.BlockSpec((DB, DR), lambda i: (0, 0))
    body = functools.partial(_rglru_step_body, n_blocks=n_blocks, bw=bw)
    return pl.pallas_call(
        body, grid=(1,),
        in_specs=[tile, pl.BlockSpec((DB, DR), lambda i: (0, 1)),
                  per_layer((CONV_W - 1, DB, DR)), per_layer((DB, DR)),
                  per_layer((CONV_W, DR)), per_layer((1, DR)),
                  per_layer((n_blocks, bw, bw)), per_layer((n_blocks, bw, bw)),
                  per_layer((1, DR)), per_layer((1, DR)), per_layer((1, DR))],
        out_specs=[tile, tile, pl.BlockSpec((CONV_W - 1, DB, DR), lambda i: (0, 0, 0))],
        out_shape=[jax.ShapeDtypeStruct((DB, DR), F32), jax.ShapeDtypeStruct((DB, DR), F32),
                   jax.ShapeDtypeStruct((CONV_W - 1, DB, DR), F32)],
        scratch_shapes=[pltpu.VMEM((DB, DR), F32)] * 3,
        compiler_params=_params(1), name="rglru_step",
    )(xrg, xrg, bufT, h0, lw["conv_w"], lw["conv_b"], lw["w_rg_a"], lw["w_rg_x"],
      lw["b_rg_a"], lw["b_rg_x"], lw["rg_lambda"])


def _merge_body(ya_ref, yr_ref, wa_ref, wr_ref, ga_ref, gr_ref, ya2_ref, yr2_ref, ga2_ref, gr2_ref,
                o_ref, o2_ref, wab_ref, wrb_ref):
    first_row_tile = pl.program_id(1) == 0

    @pl.when(first_row_tile)
    def _():
        wab_ref[...] = wa_ref[...].astype(BF16)
        wrb_ref[...] = wr_ref[...].astype(BF16)

    def merged(ya, yr, ga, gr):
        za = jnp.dot(ya, wab_ref[...], preferred_element_type=F32)
        zr = jnp.dot(yr, wrb_ref[...], preferred_element_type=F32)
        return jax.nn.sigmoid(ga) * za + jax.nn.sigmoid(gr) * zr

    o_ref[...] = merged(ya_ref[...], yr_ref[...], ga_ref[...], gr_ref[...]).astype(o_ref.dtype)

    @pl.when(first_row_tile)
    def _():
        o2_ref[...] = merged(ya2_ref[...], yr2_ref[...], ga2_ref[...], gr2_ref[...]).astype(o2_ref.dtype)


def _merge(ya, yr, gates, ya2, yr2, gates2, w_up_att, w_up_rnn, layer):
    M, att = ya.shape
    m2 = ya2.shape[0]
    dr = yr.shape[1]
    D = w_up_att.shape[-1]
    tm = min(M, 512)
    tn = _pick_tile(D, 0)
    nj = D // tn
    return pl.pallas_call(
        _merge_body, grid=(nj, M // tm),
        in_specs=[pl.BlockSpec((tm, att), lambda j, i: (i, 0)),
                  pl.BlockSpec((tm, dr), lambda j, i: (i, 0)),
                  pl.BlockSpec((None, att, tn), lambda j, i: (layer, 0, j)),
                  pl.BlockSpec((None, dr, tn), lambda j, i: (layer, 0, j)),
                  pl.BlockSpec((tm, tn), lambda j, i: (i, j)),
                  pl.BlockSpec((tm, tn), lambda j, i: (i, nj + j)),
                  pl.BlockSpec((m2, att), lambda j, i: (0, 0)),
                  pl.BlockSpec((m2, dr), lambda j, i: (0, 0)),
                  pl.BlockSpec((m2, tn), lambda j, i: (0, j)),
                  pl.BlockSpec((m2, tn), lambda j, i: (0, nj + j))],
        out_specs=[pl.BlockSpec((tm, tn), lambda j, i: (i, j)), pl.BlockSpec((m2, tn), lambda j, i: (0, j))],
        out_shape=[jax.ShapeDtypeStruct((M, D), BF16), jax.ShapeDtypeStruct((m2, D), BF16)],
        scratch_shapes=[pltpu.VMEM((att, tn), BF16), pltpu.VMEM((dr, tn), BF16)],
        compiler_params=_params(2), name="merge",
    )(ya, yr, w_up_att, w_up_rnn, gates, gates, ya2, yr2, gates2, gates2)


def _silu_bf16(c):
    return (c * jax.nn.sigmoid(c)).astype(BF16)


def _in_projection(hp, hs, w_inT, layer, dims, o_xr):
    att, idw, dr, d = dims["att"], dims["idx_width"], dims["dr"], dims["d"]
    q_scale = dims["q_scale"]
    iq_scale = dims["idx_dim"] ** -0.5
    idx_dim = dims["idx_dim"]
    mm = functools.partial(_matmul, hp, w_inT, layer, a2=hs, w_t=True)
    q = mm(0, att, [(att, BF16)], lambda acc: [acc * q_scale], name="proj_q")
    kv = mm(att, 2 * att, [(2 * att, F32), (2 * att, BF16)], lambda acc: [acc, acc], name="proj_kv")
    iq = mm(3 * att, idw, [(idw, BF16)], lambda acc: [acc * iq_scale], name="proj_iq")
    ikw = mm(3 * att + idw, 2 * LANES, [(2 * LANES, F32), (idx_dim, BF16)],
             lambda acc: [acc, acc[:, :idx_dim]], tn=2 * LANES, name="proj_ikw")
    xrg = mm(o_xr, 2 * dr, [(2 * dr, F32)], lambda acc: [acc], name="proj_rg")
    gates = mm(o_xr + 2 * dr, 2 * d, [(2 * d, F32)], lambda acc: [acc], name="proj_gate")
    return tuple(tuple(q[g]) + tuple(kv[g]) + tuple(iq[g]) + tuple(ikw[g]) + tuple(xrg[g]) + tuple(gates[g])
                 for g in range(2))


def _finish_layer(xp, xs, att_p, att_s, st_p, st_s, nxt_p, nxt_s, lw, norm_g3, layer):
    d = xp.shape[1]
    merged_p, merged_s = _merge(*att_p, *att_s, lw["w_up_att"], lw["w_up_rnn"], layer)
    (yp,), (ys,) = _matmul(merged_p, lw["w_o"], layer, 0, d, [(d, F32)], lambda acc: [acc], a2=merged_s,
                           name="proj_o")
    mid = (4 * layer + 2, 3, 4)
    xp, hp = _postnorm(xp, yp, norm_g3, 4 * layer + 1, st_p, 2, nxt=(mid[0], st_p, mid[1], mid[2]))
    xs, hs = _postnorm(xs, ys, norm_g3, 4 * layer + 1, st_s, 2, nxt=(mid[0], st_s, mid[1], mid[2]))
    dff = lw["w_mlp_in"].shape[2]
    (hid_p,), (hid_s,) = _matmul(hp, lw["w_mlp_in"], layer, 0, dff, [(dff, BF16)],
                                 lambda acc: [jnp.square(jnp.maximum(acc, 0.0))], a2=hs, name="mlp_in")
    ff_p, ff_s = _matmul_ksplit(hid_p, hid_s, lw["w_mlp_out"], layer)
    first = 4 * (layer + 1)
    xp, hp = _postnorm(xp, ff_p, norm_g3, 4 * layer + 3, st_p, 5,
                       nxt=None if nxt_p is None else (first, nxt_p, 0, 1))
    xs, hs = _postnorm(xs, ff_s, norm_g3, 4 * layer + 3, st_s, 5,
                       nxt=None if nxt_s is None else (first, nxt_s, 0, 1))
    return xp, hp, xs, hs


def kernel(x_prompt, x_sample, cache_k, cache_v, cache_idx_k, state_rglru_h, state_conv, page_table, c_prompt, c_sample, w_ada, b_ada, norm_g, w_in, rel_bias, conv_w, conv_b, w_rg_a, b_rg_a, w_rg_x, b_rg_x, rg_lambda, w_up_att, w_up_rnn, w_o, w_mlp_in, w_mlp_out):
    B, T, D = x_prompt.shape
    DB, t_new, _ = x_sample.shape
    assert t_new == 1, "the sample group decodes one token per sequence"
    L = w_ada.shape[0]
    n_pool, page, n_heads, head_dim = cache_k.shape[1:]
    att = n_heads * head_dim
    idx_dim = cache_idx_k.shape[-1]
    DR = state_rglru_h.shape[-1]
    in_w = w_in.shape[-1]
    idx_heads = (in_w - 3 * att - idx_dim - 2 * DR - 2 * D) // (idx_dim + 1)
    idw = idx_heads * idx_dim
    n_pages = page_table.shape[1]
    past = n_pages * page
    assert idx_dim == LANES and idx_heads <= LANES and n_heads <= LANES and page == LANES
    dims = dict(att=att, idx_width=idw, dr=DR, d=D, idx_dim=idx_dim, q_scale=head_dim ** -0.5 * LOG2_E)
    o_ik = 3 * att + idw
    o_xr = o_ik + idx_dim + idx_heads
    assert o_ik % (2 * LANES) == 0 and o_xr + 2 * DR + 2 * D == in_w

    tq = min(T, 256)
    nq = T // tq
    topk_p = min(TOPK_MAX, T // 4)
    topk_s = min(TOPK_MAX, (past + 1) // 4)
    Mp = B * T
    Ms = 2 * SUBLANES
    assert DB <= Ms and B + Ms <= 4 * SUBLANES and tq % LANES == 0 and T % tq == 0 and tq >= MAX_DISTANCE

    c_all = jnp.zeros((4 * SUBLANES, D), F32).at[:B].set(c_prompt).at[B:B + DB].set(c_sample)
    b_ada3 = b_ada.reshape(L, 1, 6 * D)
    norm_g3 = norm_g.reshape(L * 4, 1, D)
    lw_all = dict(conv_w=conv_w, conv_b=conv_b.reshape(L, 1, DR), w_rg_a=w_rg_a, w_rg_x=w_rg_x,
                  b_rg_a=b_rg_a.reshape(L, 1, DR), b_rg_x=b_rg_x.reshape(L, 1, DR),
                  rg_lambda=rg_lambda.reshape(L, 1, DR), w_up_att=w_up_att, w_up_rnn=w_up_rnn, w_o=w_o,
                  w_mlp_in=w_mlp_in, w_mlp_out=w_mlp_out)

    bias_tiles = _bias_tiles(rel_bias, tq)
    rel_pad = jnp.pad(rel_bias, ((0, 0), (0, LANES - n_heads)))
    w_inT = jnp.swapaxes(w_in, 1, 2)
    conv_sT = jnp.swapaxes(state_conv, 1, 2)
    zero_buf8 = jnp.zeros((B, SUBLANES, DR), F32)
    zero_h = jnp.zeros((B, 1, DR), F32)

    xp = x_prompt.reshape(Mp, D)
    xs = jnp.pad(x_sample.reshape(DB, D), ((0, Ms - DB), (0, 0)))
    outs_p = [[] for _ in range(5)]
    outs_s = [[] for _ in range(5)]

    tn_ada = _pick_tile(6 * D, 0)
    streams_p, streams_s = [], []
    for l in range(L):
        (mod,) = _matmul(c_all, w_ada, l, 0, 6 * D, [(6 * D, F32)], lambda acc, b: [acc + b],
                         extras=[(b_ada3, pl.BlockSpec((None, 1, tn_ada), lambda j, i, l=l: (l, 0, j)))],
                         a_fn=_silu_bf16, name="ada")
        streams_p.append(_Stream(mod[:B].reshape(B, 1, 6 * D), T, Mp, D))
        streams_s.append(_Stream(mod[B:B + Ms], 1, Ms, D))
    streams_p.append(None)
    streams_s.append(None)

    hp = _prenorm(xp, norm_g3, 0, streams_p[0], 0, 1)
    hs = _prenorm(xs, norm_g3, 0, streams_s[0], 0, 1)
    for l in range(L):
        proj_p, proj_s = _in_projection(hp, hs, w_inT, l, dims, o_xr)

        q, kv, kvb, iq, ikw, ikb, xrg, gates_p = proj_p
        qT = jnp.swapaxes(q.reshape(B * nq, tq, att), 1, 2)
        iqT = jnp.swapaxes(iq.reshape(B * nq, tq, idw), 1, 2)
        vT = jnp.swapaxes(kvb[:, att:].reshape(B * nq, tq, att), 1, 2)
        wT = jnp.swapaxes(ikw[:, idx_dim:].reshape(B * nq, tq, LANES), 1, 2)
        yT = _dsa_prompt(qT, kvb, vT, iqT, ikb, wT, bias_tiles, B=B, T=T, tq=tq, topk=topk_p,
                         n_heads=n_heads, head_dim=head_dim, idx_heads=idx_heads, idx_dim=idx_dim)
        ya_p = jnp.swapaxes(yT, 1, 2).reshape(Mp, att)
        yr_p, h_last, nbuf = _rglru_seq(xrg, zero_buf8, zero_h, lw_all, l, B=B, T=T, DR=DR)
        outs_p[0].append(kv[:, :att].reshape(B, T, n_heads, head_dim))
        outs_p[1].append(kv[:, att:].reshape(B, T, n_heads, head_dim))
        outs_p[2].append(ikw[:, :idx_dim].reshape(B, T, idx_dim))
        outs_p[3].append(h_last.reshape(B, DR))
        outs_p[4].append(nbuf[:, SUBLANES - (CONV_W - 1):])

        q, kv, kvb, iq, ikw, ikb, xrg, gates_s = proj_s
        k_new = kv[:DB, :att]
        v_new = kv[:DB, att:]
        ik_new = ikw[:DB, :idx_dim]
        iq3 = iq[:DB].reshape(DB, idx_heads, idx_dim)
        w3 = jnp.broadcast_to(ikw[:DB, idx_dim:idx_dim + idx_heads, None], (DB, idx_heads, page))
        new_pages = jnp.zeros((DB, page, idx_dim), F32).at[:, 0].set(ik_new)
        scores = _sample_scores(page_table, iq3, w3, cache_idx_k, new_pages, l)
        idx = _sample_topk(scores.reshape(DB, -1), topk_s)
        q_f32 = q[:DB].astype(F32) / dims["q_scale"]
        ya_s = _sample_attend(idx, page_table, q_f32.reshape(DB, n_heads, head_dim),
                              k_new.reshape(DB, n_heads, head_dim), v_new.reshape(DB, n_heads, head_dim),
                              rel_pad, cache_k, cache_v, l)
        ya_s = jnp.pad(ya_s.reshape(DB, att), ((0, Ms - DB), (0, 0))).astype(BF16)
        y_s, h_new, nbufT = _rglru_step(xrg, conv_sT, state_rglru_h, lw_all, l, DB=DB, DR=DR)
        yr_s = jnp.pad(y_s, ((0, Ms - DB), (0, 0))).astype(BF16)
        outs_s[0].append(k_new.reshape(DB, 1, n_heads, head_dim))
        outs_s[1].append(v_new.reshape(DB, 1, n_heads, head_dim))
        outs_s[2].append(ik_new.reshape(DB, 1, idx_dim))
        outs_s[3].append(h_new)
        outs_s[4].append(jnp.swapaxes(nbufT, 0, 1))

        xp, hp, xs, hs = _finish_layer(xp, xs, (ya_p, yr_p, gates_p), (ya_s, yr_s, gates_s),
                                       streams_p[l], streams_s[l], streams_p[l + 1], streams_s[l + 1],
                                       lw_all, norm_g3, l)

    y_prompt = xp.reshape(B, T, D)
    y_sample = xs[:DB].reshape(DB, 1, D)
    return (y_prompt, y_sample, *[jnp.stack(o) for o in outs_p], *[jnp.stack(o) for o in outs_s])
```

```python
import functools
import math

import jax
import jax.numpy as jnp
from jax import lax
from jax.experimental import pallas as pl
from jax.experimental.pallas import tpu as pltpu

F32 = jnp.float32
BF16 = jnp.bfloat16
I32 = jnp.int32

V7X_VMEM_BYTES = 64 * 1024 * 1024
VMEM_LIMIT = V7X_VMEM_BYTES - 8 * 1024 * 1024
LANES = 128
SUBLANES = 8

TOPK_MAX = 256
MAX_DISTANCE = 128
RG_C = 8.0
NORM_EPS = 1e-6
CONV_W = 4
MASK_NEG = -1e30
LOG2_E = math.log2(math.e)
ATTN_HEAD_GROUP = 8
INT_MIN = -2 ** 31
INT_MAX = 2 ** 31 - 1


def _params(n_axes):
    return pltpu.CompilerParams(dimension_semantics=("arbitrary",) * n_axes, vmem_limit_bytes=VMEM_LIMIT)


def _pick_tile(n, col0, candidates=(512, 256, 128)):
    for t in candidates:
        if n % t == 0 and col0 % t == 0:
            return t
    raise ValueError(f"no lane tile for width {n} at column {col0}")


def _mm_body(*refs, n_extra, n_out, cast_w, w_t, has2, a_fn, epilogue):
    a_ref, w_ref = refs[0], refs[1]
    pos = 2
    a2_ref = None
    if has2:
        a2_ref = refs[pos]
        pos += 1
    extra = refs[pos:pos + n_extra]
    pos += n_extra
    outs = refs[pos:pos + n_out]
    pos += n_out
    outs2 = refs[pos:pos + n_out] if has2 else ()
    pos += len(outs2)
    first_row_tile = pl.program_id(1) == 0
    if cast_w:
        wb_ref = refs[pos]

        @pl.when(first_row_tile)
        def _():
            wb_ref[...] = (w_ref[0] if w_t else w_ref[...]).astype(BF16)

        w_ref = wb_ref

    def project(a):
        if a_fn is not None:
            a = a_fn(a)
        contract = (((1,), (1,)), ((), ())) if w_t else (((1,), (0,)), ((), ()))
        acc = lax.dot_general(a, w_ref[...], contract, preferred_element_type=F32)
        return epilogue(acc, *[e[...] for e in extra])

    for o_ref, r in zip(outs, project(a_ref[...])):
        o_ref[...] = r.astype(o_ref.dtype)
    if has2:
        @pl.when(first_row_tile)
        def _():
            for o_ref, r in zip(outs2, project(a2_ref[...])):
                o_ref[...] = r.astype(o_ref.dtype)


def _matmul(a, w, layer, col0, n, outs, epilogue, *, a2=None, extras=(), a_fn=None, tn=None, w_t=False,
            name="mm"):
    M, K = a.shape
    tm = min(M, 1024)
    if tn is None:
        tn = _pick_tile(n, 0 if w_t else col0)
    assert M % tm == 0 and n % tn == 0
    cast_w = w.dtype != BF16
    if w_t:
        assert col0 % SUBLANES == 0 and cast_w
        w_spec = pl.BlockSpec((pl.Element(1), pl.Element(tn), pl.Element(K)),
                              lambda j, i: (layer, pl.multiple_of(col0 + j * tn, SUBLANES), 0))
    else:
        assert col0 % tn == 0
        w_spec = pl.BlockSpec((None, K, tn), lambda j, i: (layer, 0, col0 // tn + j))
    in_specs = [pl.BlockSpec((tm, K), lambda j, i: (i, 0)), w_spec]
    args = [a, w]
    out_specs = [pl.BlockSpec((tm, tn * wd // n), lambda j, i: (i, j)) for wd, _ in outs]
    out_shape = [jax.ShapeDtypeStruct((M, wd), dt) for wd, dt in outs]
    if a2 is not None:
        m2 = a2.shape[0]
        in_specs.append(pl.BlockSpec((m2, K), lambda j, i: (0, 0)))
        args.append(a2)
        out_specs += [pl.BlockSpec((m2, tn * wd // n), lambda j, i: (0, j)) for wd, _ in outs]
        out_shape += [jax.ShapeDtypeStruct((m2, wd), dt) for wd, dt in outs]
    in_specs += [spec for _, spec in extras]
    args += [arr for arr, _ in extras]
    body = functools.partial(_mm_body, n_extra=len(extras), n_out=len(outs), cast_w=cast_w, w_t=w_t,
                             has2=a2 is not None, a_fn=a_fn, epilogue=epilogue)
    res = pl.pallas_call(
        body, grid=(n // tn, M // tm), in_specs=in_specs, out_specs=out_specs, out_shape=out_shape,
        scratch_shapes=[pltpu.VMEM((tn, K) if w_t else (K, tn), BF16)] if cast_w else [],
        compiler_params=_params(2), name=name,
    )(*args)
    return res if a2 is None else (res[:len(outs)], res[len(outs):])


def _mm_ksplit_body(a_ref, w_ref, a2_ref, o_ref, o2_ref, wb_ref, *, tm):
    k = pl.program_id(2)
    i = pl.program_id(3)

    @pl.when(i == 0)
    def _():
        wb_ref[...] = w_ref[...].astype(BF16)

    def accumulate(dst, rows, a_ref):
        @pl.when(k == 0)
        def _():
            dst[rows, :] = jnp.dot(a_ref[...], wb_ref[...], preferred_element_type=F32)

        @pl.when(k > 0)
        def _():
            dst[rows, :] += jnp.dot(a_ref[...], wb_ref[...], preferred_element_type=F32)

    accumulate(o_ref, pl.ds(pl.multiple_of(i * tm, tm), tm), a_ref)

    @pl.when(i == 0)
    def _():
        accumulate(o2_ref, slice(None), a2_ref)


def _matmul_ksplit(a, a2, w, layer):
    M, K = a.shape
    m2 = a2.shape[0]
    N = w.shape[2]
    tm = min(M, 1024)
    tk = min(K, 4096)
    tn = _pick_tile(N, 0)
    n_panels = max(1, (M * tn * 4) // (8 * 1024 * 1024))
    panel = M // n_panels
    assert M % tm == 0 and K % tk == 0 and panel % tm == 0
    ni = panel // tm
    res = pl.pallas_call(
        functools.partial(_mm_ksplit_body, tm=tm), grid=(n_panels, N // tn, K // tk, ni),
        in_specs=[pl.BlockSpec((tm, tk), lambda p, j, k, i: (p * ni + i, k)),
                  pl.BlockSpec((None, tk, tn), lambda p, j, k, i: (layer, k, j)),
                  pl.BlockSpec((m2, tk), lambda p, j, k, i: (0, k))],
        out_specs=[pl.BlockSpec((panel, tn), lambda p, j, k, i: (p, j)),
                   pl.BlockSpec((m2, tn), lambda p, j, k, i: (p, j))],
        out_shape=[jax.ShapeDtypeStruct((M, N), F32), jax.ShapeDtypeStruct((n_panels * m2, N), F32)],
        scratch_shapes=[pltpu.VMEM((tk, tn), BF16)],
        compiler_params=_params(4), name="mm_ksplit",
    )(a, w, a2)
    return res[0], res[1][:m2]


def _rms(x, g):
    return x * lax.rsqrt(jnp.mean(x * x, axis=-1, keepdims=True) + NORM_EPS) * g


def _prenorm_body(x_ref, g_ref, sh_ref, sc_ref, h_ref):
    h = _rms(x_ref[...], g_ref[...]) * (1.0 + sc_ref[...]) + sh_ref[...]
    h_ref[...] = h.astype(h_ref.dtype)


def _postnorm_body(*refs, with_next):
    if with_next:
        x_ref, y_ref, gpost_ref, gate_ref, gpre_ref, sh_ref, sc_ref, xo_ref, h_ref = refs
    else:
        x_ref, y_ref, gpost_ref, gate_ref, xo_ref = refs
    x = x_ref[...] + gate_ref[...] * _rms(y_ref[...], gpost_ref[...])
    xo_ref[...] = x
    if with_next:
        h = _rms(x, gpre_ref[...]) * (1.0 + sc_ref[...]) + sh_ref[...]
        h_ref[...] = h.astype(h_ref.dtype)


class _Stream:
    def __init__(self, mod, rows_per_seq, n_rows, d):
        self.mod = mod
        self.rows_per_seq = rows_per_seq
        self.tm = min(256, n_rows)
        self.d = d

    def mod_spec(self, chunk):
        tm, d = self.tm, self.d
        if self.rows_per_seq > 1:
            rps = self.rows_per_seq
            return pl.BlockSpec((None, 1, d), lambda i: (i * tm // rps, 0, chunk))
        return pl.BlockSpec((tm, d), lambda i: (i, chunk))


def _norm_spec(g_index, d):
    return pl.BlockSpec((None, 1, d), lambda i: (g_index, 0, 0))


def _prenorm(x, norm_g3, g_index, st, shift_chunk, scale_chunk):
    M, D = x.shape
    tm = st.tm
    row = pl.BlockSpec((tm, D), lambda i: (i, 0))
    return pl.pallas_call(
        _prenorm_body, grid=(M // tm,),
        in_specs=[row, _norm_spec(g_index, D), st.mod_spec(shift_chunk), st.mod_spec(scale_chunk)],
        out_specs=row, out_shape=jax.ShapeDtypeStruct((M, D), BF16),
        compiler_params=_params(1), name="prenorm",
    )(x, norm_g3, st.mod, st.mod)


def _postnorm(x, y, norm_g3, g_post, st, gate_chunk, nxt=None):
    M, D = x.shape
    tm = st.tm
    row = pl.BlockSpec((tm, D), lambda i: (i, 0))
    in_specs = [row, row, _norm_spec(g_post, D), st.mod_spec(gate_chunk)]
    args = [x, y, norm_g3, st.mod]
    out_specs = [row]
    out_shape = [jax.ShapeDtypeStruct((M, D), F32)]
    if nxt is not None:
        g_pre, st_n, sh_c, sc_c = nxt
        in_specs += [_norm_spec(g_pre, D), st_n.mod_spec(sh_c), st_n.mod_spec(sc_c)]
        args += [norm_g3, st_n.mod, st_n.mod]
        out_specs.append(row)
        out_shape.append(jax.ShapeDtypeStruct((M, D), BF16))
    res = pl.pallas_call(
        functools.partial(_postnorm_body, with_next=nxt is not None), grid=(M // tm,),
        in_specs=in_specs, out_specs=out_specs, out_shape=out_shape,
        compiler_params=_params(1), name="postnorm",
    )(*args)
    return res if nxt is not None else (res[0], None)


def _t5_bucket(dist, n_buckets):
    max_exact = n_buckets // 2
    d = jnp.maximum(dist, 0)
    d_f = jnp.maximum(d, 1).astype(F32)
    large = max_exact + (jnp.log(d_f / max_exact) / math.log(MAX_DISTANCE / max_exact)
                         * (n_buckets - max_exact)).astype(I32)
    large = jnp.minimum(large, n_buckets - 1)
    return jnp.where(d < max_exact, d, large)


def _bias_tiles_body(rel_ref, o_ref, *, tq, n_buckets):
    which = pl.program_id(0)
    h = pl.program_id(1)
    r = lax.broadcasted_iota(I32, (tq, tq), 0)
    c = lax.broadcasted_iota(I32, (tq, tq), 1)
    bucket = _t5_bucket(which * tq + c - r, n_buckets)
    far = rel_ref[n_buckets - 1, h]
    acc = jnp.zeros((tq, tq), F32)
    for n in range(n_buckets - 1):
        acc = jnp.where(bucket == n, rel_ref[n, h] - far, acc)
    o_ref[...] = acc * LOG2_E


def _bias_tiles(rel_bias, tq):
    nb, n_heads = rel_bias.shape
    return pl.pallas_call(
        functools.partial(_bias_tiles_body, tq=tq, n_buckets=nb), grid=(2, n_heads),
        in_specs=[pl.BlockSpec(memory_space=pltpu.SMEM)],
        out_specs=pl.BlockSpec((None, None, tq, tq), lambda w, h: (w, h, 0, 0)),
        out_shape=jax.ShapeDtypeStruct((2, n_heads, tq, tq), F32),
        compiler_params=_params(2), name="bias_tiles",
    )(rel_bias)


def _attn_body(qT_ref, k_ref, vT_ref, iqT_ref, ik_ref, wT_ref, bias_ref, o_ref,
               key_ref, mask_ref, p_ref, m_ref, l_ref, acc_ref, lg_ref,
               *, topk, n_heads, head_dim, idx_heads, idx_dim, tq, n_chunks):
    i = pl.program_id(1)
    nck = i + 1
    w_scale = idx_heads ** -0.5
    q_pos = i * tq + lax.broadcasted_iota(I32, (tq, tq), 1)
    r_loc = lax.broadcasted_iota(I32, (tq, tq), 0)

    def rows(c):
        return pl.ds(pl.multiple_of(c * tq, tq), tq)

    def score_chunk(c, carry):
        ik_c = ik_ref[rows(c), :]
        sc = None
        for h in range(idx_heads):
            s = jnp.dot(ik_c, iqT_ref[h * idx_dim:(h + 1) * idx_dim, :], preferred_element_type=F32)
            contrib = jnp.maximum(s, 0.0) * (wT_ref[h:h + 1, :] * w_scale)
            sc = contrib if sc is None else sc + contrib
        bits = pltpu.bitcast(sc + 0.0, I32)
        key = jnp.where(bits < 0, bits ^ INT_MAX, bits)
        key_ref[rows(c), :] = jnp.where(c * tq + r_loc <= q_pos, key, INT_MIN)
        return carry

    lax.fori_loop(0, nck, score_chunk, 0)

    def count(pred):
        def body(c, cnt):
            hit = pred(key_ref[rows(c), :], c).astype(I32).reshape(tq // SUBLANES, SUBLANES, tq)
            return cnt + jnp.sum(hit, axis=0)
        cnt = lax.fori_loop(0, nck, body, jnp.zeros((SUBLANES, tq), I32))
        return jnp.sum(cnt, axis=0, keepdims=True)

    thr = jnp.where(count(lambda k, c: k >= 0) >= topk, 0, INT_MIN).astype(I32)

    def thr_bit(s, thr):
        cand = thr | jnp.left_shift(jnp.int32(1), 30 - s)
        return jnp.where(count(lambda k, c: k >= cand) >= topk, cand, thr)

    thr = lax.fori_loop(0, 31, thr_bit, thr)

    n_gt = count(lambda k, c: k > thr)
    n_ge = count(lambda k, c: k >= thr)
    need = topk - n_gt
    p_ref[...] = jnp.full((1, tq), INT_MAX, I32)
    has_ties = jnp.max(jnp.where((n_ge > topk) & (thr > INT_MIN), 1, 0)) > 0

    @pl.when(has_ties)
    def _():
        n_bits = max(1, (n_chunks * tq - 1).bit_length())

        def pos_bit(s, p):
            cand = p | jnp.left_shift(jnp.int32(1), n_bits - 1 - s)
            below = count(lambda k, c: (k == thr) & (c * tq + r_loc < cand))
            return jnp.where(below < need, cand, p)

        p_ref[...] = lax.fori_loop(0, n_bits, pos_bit, jnp.zeros((1, tq), I32))

    p_last = p_ref[...]

    def mask_chunk(c, carry):
        k = key_ref[rows(c), :]
        pos = c * tq + r_loc
        keep = ((k > thr) | ((k == thr) & (pos <= p_last))) & (pos <= q_pos)
        mask_ref[c] = jnp.where(keep, 0.0, MASK_NEG)
        return carry

    lax.fori_loop(0, nck, mask_chunk, 0)

    m_ref[...] = jnp.full(m_ref.shape, MASK_NEG, F32)
    l_ref[...] = jnp.zeros(l_ref.shape, F32)
    acc_ref[...] = jnp.zeros(acc_ref.shape, F32)
    group = lg_ref.shape[0]

    def logits(c, carry, h0, near):
        mask_c = mask_ref[c]
        for g in range(group):
            h = h0 + g
            hs = slice(h * head_dim, (h + 1) * head_dim)
            s = jnp.dot(k_ref[rows(c), hs], qT_ref[hs, :], preferred_element_type=F32) + mask_c
            if near:
                s = s + bias_ref[i - c, h]
            lg_ref[g, rows(c), :] = s
            m_ref[h] = jnp.maximum(m_ref[h], jnp.max(s, axis=0, keepdims=True))
        return carry

    def values(c, carry, h0):
        for g in range(group):
            h = h0 + g
            hs = slice(h * head_dim, (h + 1) * head_dim)
            p = jnp.exp2(lg_ref[g, rows(c), :] - m_ref[h])
            l_ref[h] += jnp.sum(p, axis=0, keepdims=True)
            acc_ref[h] += jnp.dot(vT_ref[c, hs, :], p.astype(BF16), preferred_element_type=F32)
        return carry

    c_near = jnp.maximum(i - 1, 0)
    for h0 in range(0, n_heads, group):
        lax.fori_loop(0, c_near, functools.partial(logits, h0=h0, near=False), 0)
        lax.fori_loop(c_near, nck, functools.partial(logits, h0=h0, near=True), 0)
        lax.fori_loop(0, nck, functools.partial(values, h0=h0), 0)
    for h in range(n_heads):
        hs = slice(h * head_dim, (h + 1) * head_dim)
        o_ref[hs, :] = (acc_ref[h] / l_ref[h]).astype(o_ref.dtype)


def _dsa_prompt(qT, kb, vT, iqT, ikb, wT, bias, *, B, T, tq, topk, n_heads, head_dim, idx_heads, idx_dim):
    nq = T // tq
    att = n_heads * head_dim
    once = dict(pipeline_mode=pl.Buffered(1))
    body = functools.partial(_attn_body, topk=topk, n_heads=n_heads, head_dim=head_dim,
                             idx_heads=idx_heads, idx_dim=idx_dim, tq=tq, n_chunks=nq)
    return pl.pallas_call(
        body, grid=(B, nq),
        in_specs=[
            pl.BlockSpec((None, att, tq), lambda b, i: (b * nq + i, 0, 0)),
            pl.BlockSpec((T, att), lambda b, i: (b, 0), **once),
            pl.BlockSpec((nq, att, tq), lambda b, i: (b, 0, 0), **once),
            pl.BlockSpec((None, idx_heads * idx_dim, tq), lambda b, i: (b * nq + i, 0, 0)),
            pl.BlockSpec((T, idx_dim), lambda b, i: (b, 0), **once),
            pl.BlockSpec((None, wT.shape[1], tq), lambda b, i: (b * nq + i, 0, 0)),
            pl.BlockSpec((2, n_heads, tq, tq), lambda b, i: (0, 0, 0, 0), **once),
        ],
        out_specs=pl.BlockSpec((None, att, tq), lambda b, i: (b * nq + i, 0, 0)),
        out_shape=jax.ShapeDtypeStruct((B * nq, att, tq), BF16),
        scratch_shapes=[pltpu.VMEM((T, tq), I32), pltpu.VMEM((nq, tq, tq), F32), pltpu.VMEM((1, tq), I32),
                        pltpu.VMEM((n_heads, 1, tq), F32), pltpu.VMEM((n_heads, 1, tq), F32),
                        pltpu.VMEM((n_heads, head_dim, tq), F32),
                        pltpu.VMEM((math.gcd(ATTN_HEAD_GROUP, n_heads), T, tq), F32)],
        compiler_params=_params(2), name="dsa_prompt",
    )(qT, kb, vT, iqT, ikb, wT, bias)


def _sample_scores_body(pt_ref, iq_ref, w_ref, *refs, n_groups, idx_heads):
    page_refs, new_ref, o_ref = refs[:SUBLANES], refs[SUBLANES], refs[SUBLANES + 1]
    j = pl.program_id(1)

    def page_scores(keys):
        s = lax.dot_general(iq_ref[...], keys.astype(BF16), (((1,), (1,)), ((), ())),
                            preferred_element_type=F32)
        w = w_ref[...] * idx_heads ** -0.5
        return jnp.sum(jnp.maximum(s, 0.0) * w, axis=0, keepdims=True) + 0.0

    group = pl.ds(pl.multiple_of(j * SUBLANES, SUBLANES), SUBLANES)

    @pl.when(j < n_groups)
    def _():
        o_ref[group, :] = jnp.concatenate([page_scores(r[...]) for r in page_refs], axis=0)

    @pl.when(j == n_groups)
    def _():
        sc = jnp.broadcast_to(page_scores(new_ref[...]), (SUBLANES, new_ref.shape[0]))
        lane = lax.broadcasted_iota(I32, sc.shape, 1)
        row = lax.broadcasted_iota(I32, sc.shape, 0)
        o_ref[group, :] = jnp.where((lane == 0) & (row == 0), sc, -jnp.inf)


def _sample_scores(page_table, iq3, w3, cache_idx_k, new_pages, layer):
    DB, n_pages = page_table.shape
    _, idx_heads, idx_dim = iq3.shape
    page = cache_idx_k.shape[2]
    assert n_pages % SUBLANES == 0
    n_groups = n_pages // SUBLANES
    rows = n_pages + SUBLANES

    def page_spec(r):
        return pl.BlockSpec(
            (None, None, page, idx_dim),
            lambda b, j, pt: (layer, pt[b, jnp.minimum(j, n_groups - 1) * SUBLANES + r], 0, 0))

    grid_spec = pltpu.PrefetchScalarGridSpec(
        num_scalar_prefetch=1, grid=(DB, n_groups + 1),
        in_specs=[pl.BlockSpec((None, idx_heads, idx_dim), lambda b, j, pt: (b, 0, 0)),
                  pl.BlockSpec((None, idx_heads, page), lambda b, j, pt: (b, 0, 0))]
        + [page_spec(r) for r in range(SUBLANES)]
        + [pl.BlockSpec((None, page, idx_dim), lambda b, j, pt: (b, 0, 0))],
        out_specs=pl.BlockSpec((None, rows, page), lambda b, j, pt: (b, 0, 0)),
    )
    return pl.pallas_call(
        functools.partial(_sample_scores_body, n_groups=n_groups, idx_heads=idx_heads),
        grid_spec=grid_spec, out_shape=jax.ShapeDtypeStruct((DB, rows, page), F32),
        compiler_params=_params(2), name="sample_scores",
    )(page_table, iq3, w3, *([cache_idx_k] * SUBLANES), new_pages)


def _topk_body(s_ref, o_ref, work_ref, *, topk):
    work_ref[...] = s_ref[...]
    lane = lax.broadcasted_iota(I32, s_ref.shape, 1)
    slot = lax.broadcasted_iota(I32, o_ref.shape, 1)

    def body(k, acc):
        s = work_ref[...]
        m = jnp.max(s, axis=1, keepdims=True)
        idx = jnp.min(jnp.where(s == m, lane, INT_MAX), axis=1, keepdims=True)
        work_ref[...] = jnp.where(lane == idx, -jnp.inf, s)
        return jnp.where(slot == k, idx, acc)

    o_ref[...] = lax.fori_loop(0, topk, body, jnp.zeros(o_ref.shape, I32))


def _sample_topk(scores2d, topk):
    DB, n = scores2d.shape
    return pl.pallas_call(
        functools.partial(_topk_body, topk=topk),
        out_shape=jax.ShapeDtypeStruct((DB, topk), I32),
        scratch_shapes=[pltpu.VMEM((DB, n), F32)],
        compiler_params=pltpu.CompilerParams(vmem_limit_bytes=VMEM_LIMIT), name="sample_topk",
    )(scores2d)


def _sample_attend_body(idx_sm, pt_sm, q_ref, knew_ref, vnew_ref, idxc_ref, rel_ref, ck_hbm, cv_hbm, o_ref,
                        kbuf, vbuf, sem, *, layer, past, page, topk, n_heads, head_dim, n_buckets):
    b = pl.program_id(0)

    def row_copies(k):
        p = jnp.minimum(idx_sm[b, k], past - 1)
        phys = pt_sm[b, p // page]
        off = p % page
        return (pltpu.make_async_copy(ck_hbm.at[layer, phys, off], kbuf.at[:, k, :], sem.at[0]),
                pltpu.make_async_copy(cv_hbm.at[layer, phys, off], vbuf.at[:, k, :], sem.at[1]))

    def start(k, carry):
        for cp in row_copies(k):
            cp.start()
        return carry

    def wait(k, carry):
        for cp in row_copies(k):
            cp.wait()
        return carry

    lax.fori_loop(0, topk, start, 0)
    lax.fori_loop(0, topk, wait, 0)

    idxc = idxc_ref[...]
    is_new = idxc >= past
    bucket = _t5_bucket(past - idxc, n_buckets)
    bias = jnp.zeros((topk, LANES), F32)
    for n in range(n_buckets):
        bias = jnp.where(bucket == n, rel_ref[n:n + 1, :], bias)
    for h in range(n_heads):
        row_h = slice(h, h + 1)
        ks = jnp.where(is_new, knew_ref[row_h, :], kbuf[h])
        vs = jnp.where(is_new, vnew_ref[row_h, :], vbuf[h])
        s = jnp.sum(ks * q_ref[row_h, :], axis=1, keepdims=True) * head_dim ** -0.5 + bias[:, h:h + 1]
        e = jnp.exp(s - jnp.max(s, axis=0, keepdims=True))
        den = jnp.sum(e, axis=0, keepdims=True)
        o_ref[row_h, :] = jnp.sum(e * vs, axis=0, keepdims=True) / den


def _sample_attend(idx, page_table, q3, knew3, vnew3, rel_pad, cache_k, cache_v, layer):
    DB, topk = idx.shape
    page, n_heads, head_dim = cache_k.shape[2:]
    past = page_table.shape[1] * page
    row = pl.BlockSpec((None, n_heads, head_dim), lambda b, ix, pt: (b, 0, 0))
    grid_spec = pltpu.PrefetchScalarGridSpec(
        num_scalar_prefetch=2, grid=(DB,),
        in_specs=[row, row, row,
                  pl.BlockSpec((None, topk, 1), lambda b, ix, pt: (b, 0, 0)),
                  pl.BlockSpec(rel_pad.shape, lambda b, ix, pt: (0, 0)),
                  pl.BlockSpec(memory_space=pl.ANY), pl.BlockSpec(memory_space=pl.ANY)],
        out_specs=row,
        scratch_shapes=[pltpu.VMEM((n_heads, topk, head_dim), F32), pltpu.VMEM((n_heads, topk, head_dim), F32),
                        pltpu.SemaphoreType.DMA((2,))],
    )
    body = functools.partial(_sample_attend_body, layer=layer, past=past, page=page, topk=topk,
                             n_heads=n_heads, head_dim=head_dim, n_buckets=rel_pad.shape[0])
    return pl.pallas_call(
        body, grid_spec=grid_spec, out_shape=jax.ShapeDtypeStruct((DB, n_heads, head_dim), F32),
        compiler_params=_params(1), name="sample_attend",
    )(idx, page_table, q3, knew3, vnew3, idx.reshape(DB, topk, 1), rel_pad, cache_k, cache_v)


def _softplus(z):
    return jnp.maximum(z, 0.0) + jnp.log1p(jnp.exp(-jnp.abs(z)))


def _one_minus_exp(x):
    e = jnp.exp(x)
    log_e = jnp.log(e)
    return jnp.where(e == 1.0, -x, (1.0 - e) * x / jnp.where(log_e == 0.0, 1.0, log_e))


def _rglru_gates(xc_ref, a_ref, u_ref, wa_ref, wx_ref, ba_ref, bx_ref, lam_ref, n_blocks, bw):
    for n in range(n_blocks):
        cs = slice(n * bw, (n + 1) * bw)
        xb = xc_ref[:, cs]
        xb16 = xb.astype(BF16)
        r = jax.nn.sigmoid(jnp.dot(xb16, wa_ref[n].astype(BF16), preferred_element_type=F32) + ba_ref[:, cs])
        ig = jax.nn.sigmoid(jnp.dot(xb16, wx_ref[n].astype(BF16), preferred_element_type=F32) + bx_ref[:, cs])
        log_a = -RG_C * r * _softplus(-lam_ref[:, cs])
        a_ref[:, cs] = jnp.exp(log_a)
        u_ref[:, cs] = jnp.sqrt(_one_minus_exp(2.0 * log_a)) * ig * xb


def _rglru_seq_body(xr_ref, xg_ref, buf_ref, h0_ref, cw_ref, cb_ref, wa_ref, wx_ref, ba_ref, bx_ref, lam_ref,
                    y_ref, hl_ref, nb_ref, xext_ref, xc_ref, a_ref, u_ref, hc_ref, *, tt, n_blocks, bw):
    i = pl.program_id(1)
    halo = SUBLANES

    @pl.when(i == 0)
    def _():
        xext_ref[0:halo, :] = buf_ref[...]
        hc_ref[...] = h0_ref[...]

    @pl.when(i > 0)
    def _():
        xext_ref[0:halo, :] = xext_ref[tt:tt + halo, :]

    xext_ref[halo:halo + tt, :] = xr_ref[...]
    xc = cb_ref[...] + xext_ref[halo - 3:halo - 3 + tt, :] * cw_ref[0:1, :]
    for j in range(1, CONV_W):
        xc = xc + xext_ref[halo - 3 + j:halo - 3 + j + tt, :] * cw_ref[j:j + 1, :]
    xc_ref[...] = xc
    _rglru_gates(xc_ref, a_ref, u_ref, wa_ref, wx_ref, ba_ref, bx_ref, lam_ref, n_blocks, bw)

    def step(t, h):
        row = pl.ds(t, 1)
        h = a_ref[row, :] * h + u_ref[row, :]
        u_ref[row, :] = h
        return h

    h = lax.fori_loop(0, tt, step, hc_ref[...], unroll=8)
    hc_ref[...] = h
    y_ref[...] = (u_ref[...] * jax.nn.gelu(xg_ref[...])).astype(y_ref.dtype)

    @pl.when(i == pl.num_programs(1) - 1)
    def _():
        hl_ref[...] = h
        nb_ref[...] = xext_ref[tt:tt + halo, :]


def _rglru_seq(xrg, buf8, h0, lw, layer, *, B, T, DR):
    tt = min(T, 256)
    nt = T // tt
    n_blocks, bw = lw["w_rg_a"].shape[1:3]

    def per_layer(shape):
        return pl.BlockSpec((None,) + shape, lambda b, i: (layer,) + (0,) * len(shape))

    per_seq8 = pl.BlockSpec((None, SUBLANES, DR), lambda b, i: (b, 0, 0))
    per_seq1 = pl.BlockSpec((None, 1, DR), lambda b, i: (b, 0, 0))
    body = functools.partial(_rglru_seq_body, tt=tt, n_blocks=n_blocks, bw=bw)
    return pl.pallas_call(
        body, grid=(B, nt),
        in_specs=[pl.BlockSpec((tt, DR), lambda b, i: (b * nt + i, 0)),
                  pl.BlockSpec((tt, DR), lambda b, i: (b * nt + i, 1)),
                  per_seq8, per_seq1,
                  per_layer((CONV_W, DR)), per_layer((1, DR)),
                  per_layer((n_blocks, bw, bw)), per_layer((n_blocks, bw, bw)),
                  per_layer((1, DR)), per_layer((1, DR)), per_layer((1, DR))],
        out_specs=[pl.BlockSpec((tt, DR), lambda b, i: (b * nt + i, 0)), per_seq1, per_seq8],
        out_shape=[jax.ShapeDtypeStruct((B * T, DR), BF16), jax.ShapeDtypeStruct((B, 1, DR), F32),
                   jax.ShapeDtypeStruct((B, SUBLANES, DR), F32)],
        scratch_shapes=[pltpu.VMEM((tt + SUBLANES, DR), F32), pltpu.VMEM((tt, DR), F32),
                        pltpu.VMEM((tt, DR), F32), pltpu.VMEM((tt, DR), F32), pltpu.VMEM((1, DR), F32)],
        compiler_params=_params(2), name="rglru_seq",
    )(xrg, xrg, buf8, h0, lw["conv_w"], lw["conv_b"], lw["w_rg_a"], lw["w_rg_x"],
      lw["b_rg_a"], lw["b_rg_x"], lw["rg_lambda"])


def _rglru_step_body(xr_ref, xg_ref, buf_ref, h0_ref, cw_ref, cb_ref, wa_ref, wx_ref, ba_ref, bx_ref, lam_ref,
                     y_ref, hl_ref, nb_ref, xc_ref, a_ref, u_ref, *, n_blocks, bw):
    xr = xr_ref[...]
    xc = cb_ref[...] + xr * cw_ref[CONV_W - 1:CONV_W, :]
    for j in range(CONV_W - 1):
        xc = xc + buf_ref[j] * cw_ref[j:j + 1, :]
    xc_ref[...] = xc
    _rglru_gates(xc_ref, a_ref, u_ref, wa_ref, wx_ref, ba_ref, bx_ref, lam_ref, n_blocks, bw)
    h = a_ref[...] * h0_ref[...] + u_ref[...]
    hl_ref[...] = h
    y_ref[...] = h * jax.nn.gelu(xg_ref[...])
    for j in range(CONV_W - 2):
        nb_ref[j] = buf_ref[j + 1]
    nb_ref[CONV_W - 2] = xr


def _rglru_step(xrg, bufT, h0, lw, layer, *, DB, DR):
    n_blocks, bw = lw["w_rg_a"].shape[1:3]

    def per_layer(shape):
        return pl.BlockSpec((None,) + shape, lambda i: (layer,) + (0,) * len(shape))

    tile = pl.BlockSpec((DB, DR), lambda i: (0, 0))
    body = functools.partial(_rglru_step_body, n_blocks=n_blocks, bw=bw)
    return pl.pallas_call(
        body, grid=(1,),
        in_specs=[tile, pl.BlockSpec((DB, DR), lambda i: (0, 1)),
                  per_layer((CONV_W - 1, DB, DR)), per_layer((DB, DR)),
                  per_layer((CONV_W, DR)), per_layer((1, DR)),
                  per_layer((n_blocks, bw, bw)), per_layer((n_blocks, bw, bw)),
                  per_layer((1, DR)), per_layer((1, DR)), per_layer((1, DR))],
        out_specs=[tile, tile, pl.BlockSpec((CONV_W - 1, DB, DR), lambda i: (0, 0, 0))],
        out_shape=[jax.ShapeDtypeStruct((DB, DR), F32), jax.ShapeDtypeStruct((DB, DR), F32),
                   jax.ShapeDtypeStruct((CONV_W - 1, DB, DR), F32)],
        scratch_shapes=[pltpu.VMEM((DB, DR), F32)] * 3,
        compiler_params=_params(1), name="rglru_step",
    )(xrg, xrg, bufT, h0, lw["conv_w"], lw["conv_b"], lw["w_rg_a"], lw["w_rg_x"],
      lw["b_rg_a"], lw["b_rg_x"], lw["rg_lambda"])


def _merge_body(ya_ref, yr_ref, wa_ref, wr_ref, ga_ref, gr_ref, ya2_ref, yr2_ref, ga2_ref, gr2_ref,
                o_ref, o2_ref, wab_ref, wrb_ref):
    first_row_tile = pl.program_id(1) == 0

    @pl.when(first_row_tile)
    def _():
        wab_ref[...] = wa_ref[...].astype(BF16)
        wrb_ref[...] = wr_ref[...].astype(BF16)

    def merged(ya, yr, ga, gr):
        za = jnp.dot(ya, wab_ref[...], preferred_element_type=F32)
        zr = jnp.dot(yr, wrb_ref[...], preferred_element_type=F32)
        return jax.nn.sigmoid(ga) * za + jax.nn.sigmoid(gr) * zr

    o_ref[...] = merged(ya_ref[...], yr_ref[...], ga_ref[...], gr_ref[...]).astype(o_ref.dtype)

    @pl.when(first_row_tile)
    def _():
        o2_ref[...] = merged(ya2_ref[...], yr2_ref[...], ga2_ref[...], gr2_ref[...]).astype(o2_ref.dtype)


def _merge(ya, yr, gates, ya2, yr2, gates2, w_up_att, w_up_rnn, layer):
    M, att = ya.shape
    m2 = ya2.shape[0]
    dr = yr.shape[1]
    D = w_up_att.shape[-1]
    tm = min(M, 512)
    tn = _pick_tile(D, 0)
    nj = D // tn
    return pl.pallas_call(
        _merge_body, grid=(nj, M // tm),
        in_specs=[pl.BlockSpec((tm, att), lambda j, i: (i, 0)),
                  pl.BlockSpec((tm, dr), lambda j, i: (i, 0)),
                  pl.BlockSpec((None, att, tn), lambda j, i: (layer, 0, j)),
                  pl.BlockSpec((None, dr, tn), lambda j, i: (layer, 0, j)),
                  pl.BlockSpec((tm, tn), lambda j, i: (i, j)),
                  pl.BlockSpec((tm, tn), lambda j, i: (i, nj + j)),
                  pl.BlockSpec((m2, att), lambda j, i: (0, 0)),
                  pl.BlockSpec((m2, dr), lambda j, i: (0, 0)),
                  pl.BlockSpec((m2, tn), lambda j, i: (0, j)),
                  pl.BlockSpec((m2, tn), lambda j, i: (0, nj + j))],
        out_specs=[pl.BlockSpec((tm, tn), lambda j, i: (i, j)), pl.BlockSpec((m2, tn), lambda j, i: (0, j))],
        out_shape=[jax.ShapeDtypeStruct((M, D), BF16), jax.ShapeDtypeStruct((m2, D), BF16)],
        scratch_shapes=[pltpu.VMEM((att, tn), BF16), pltpu.VMEM((dr, tn), BF16)],
        compiler_params=_params(2), name="merge",
    )(ya, yr, w_up_att, w_up_rnn, gates, gates, ya2, yr2, gates2, gates2)


def _silu_bf16(c):
    return (c * jax.nn.sigmoid(c)).astype(BF16)


def _in_projection(hp, hs, w_inT, layer, dims, o_xr):
    att, idw, dr, d = dims["att"], dims["idx_width"], dims["dr"], dims["d"]
    q_scale = dims["q_scale"]
    iq_scale = dims["idx_dim"] ** -0.5
    idx_dim = dims["idx_dim"]
    mm = functools.partial(_matmul, hp, w_inT, layer, a2=hs, w_t=True)
    q = mm(0, att, [(att, BF16)], lambda acc: [acc * q_scale], name="proj_q")
    kv = mm(att, 2 * att, [(2 * att, F32), (2 * att, BF16)], lambda acc: [acc, acc], name="proj_kv")
    iq = mm(3 * att, idw, [(idw, BF16)], lambda acc: [acc * iq_scale], name="proj_iq")
    ikw = mm(3 * att + idw, 2 * LANES, [(2 * LANES, F32), (idx_dim, BF16)],
             lambda acc: [acc, acc[:, :idx_dim]], tn=2 * LANES, name="proj_ikw")
    xrg = mm(o_xr, 2 * dr, [(2 * dr, F32)], lambda acc: [acc], name="proj_rg")
    gates = mm(o_xr + 2 * dr, 2 * d, [(2 * d, F32)], lambda acc: [acc], name="proj_gate")
    return tuple(tuple(q[g]) + tuple(kv[g]) + tuple(iq[g]) + tuple(ikw[g]) + tuple(xrg[g]) + tuple(gates[g])
                 for g in range(2))


def _finish_layer(xp, xs, att_p, att_s, st_p, st_s, nxt_p, nxt_s, lw, norm_g3, layer):
    d = xp.shape[1]
    merged_p, merged_s = _merge(*att_p, *att_s, lw["w_up_att"], lw["w_up_rnn"], layer)
    (yp,), (ys,) = _matmul(merged_p, lw["w_o"], layer, 0, d, [(d, F32)], lambda acc: [acc], a2=merged_s,
                           name="proj_o")
    mid = (4 * layer + 2, 3, 4)
    xp, hp = _postnorm(xp, yp, norm_g3, 4 * layer + 1, st_p, 2, nxt=(mid[0], st_p, mid[1], mid[2]))
    xs, hs = _postnorm(xs, ys, norm_g3, 4 * layer + 1, st_s, 2, nxt=(mid[0], st_s, mid[1], mid[2]))
    dff = lw["w_mlp_in"].shape[2]
    (hid_p,), (hid_s,) = _matmul(hp, lw["w_mlp_in"], layer, 0, dff, [(dff, BF16)],
                                 lambda acc: [jnp.square(jnp.maximum(acc, 0.0))], a2=hs, name="mlp_in")
    ff_p, ff_s = _matmul_ksplit(hid_p, hid_s, lw["w_mlp_out"], layer)
    first = 4 * (layer + 1)
    xp, hp = _postnorm(xp, ff_p, norm_g3, 4 * layer + 3, st_p, 5,
                       nxt=None if nxt_p is None else (first, nxt_p, 0, 1))
    xs, hs = _postnorm(xs, ff_s, norm_g3, 4 * layer + 3, st_s, 5,
                       nxt=None if nxt_s is None else (first, nxt_s, 0, 1))
    return xp, hp, xs, hs


def kernel(x_prompt, x_sample, cache_k, cache_v, cache_idx_k, state_rglru_h, state_conv, page_table, c_prompt, c_sample, w_ada, b_ada, norm_g, w_in, rel_bias, conv_w, conv_b, w_rg_a, b_rg_a, w_rg_x, b_rg_x, rg_lambda, w_up_att, w_up_rnn, w_o, w_mlp_in, w_mlp_out):
    B, T, D = x_prompt.shape
    DB, t_new, _ = x_sample.shape
    assert t_new == 1, "the sample group decodes one token per sequence"
    L = w_ada.shape[0]
    n_pool, page, n_heads, head_dim = cache_k.shape[1:]
    att = n_heads * head_dim
    idx_dim = cache_idx_k.shape[-1]
    DR = state_rglru_h.shape[-1]
    in_w = w_in.shape[-1]
    idx_heads = (in_w - 3 * att - idx_dim - 2 * DR - 2 * D) // (idx_dim + 1)
    idw = idx_heads * idx_dim
    n_pages = page_table.shape[1]
    past = n_pages * page
    assert idx_dim == LANES and idx_heads <= LANES and n_heads <= LANES and page == LANES
    dims = dict(att=att, idx_width=idw, dr=DR, d=D, idx_dim=idx_dim, q_scale=head_dim ** -0.5 * LOG2_E)
    o_ik = 3 * att + idw
    o_xr = o_ik + idx_dim + idx_heads
    assert o_ik % (2 * LANES) == 0 and o_xr + 2 * DR + 2 * D == in_w

    tq = min(T, 256)
    nq = T // tq
    topk_p = min(TOPK_MAX, T // 4)
    topk_s = min(TOPK_MAX, (past + 1) // 4)
    Mp = B * T
    Ms = 2 * SUBLANES
    assert DB <= Ms and B + Ms <= 4 * SUBLANES and tq % LANES == 0 and T % tq == 0 and tq >= MAX_DISTANCE

    c_all = jnp.zeros((4 * SUBLANES, D), F32).at[:B].set(c_prompt).at[B:B + DB].set(c_sample)
    b_ada3 = b_ada.reshape(L, 1, 6 * D)
    norm_g3 = norm_g.reshape(L * 4, 1, D)
    lw_all = dict(conv_w=conv_w, conv_b=conv_b.reshape(L, 1, DR), w_rg_a=w_rg_a, w_rg_x=w_rg_x,
                  b_rg_a=b_rg_a.reshape(L, 1, DR), b_rg_x=b_rg_x.reshape(L, 1, DR),
                  rg_lambda=rg_lambda.reshape(L, 1, DR), w_up_att=w_up_att, w_up_rnn=w_up_rnn, w_o=w_o,
                  w_mlp_in=w_mlp_in, w_mlp_out=w_mlp_out)

    bias_tiles = _bias_tiles(rel_bias, tq)
    rel_pad = jnp.pad(rel_bias, ((0, 0), (0, LANES - n_heads)))
    w_inT = jnp.swapaxes(w_in, 1, 2)
    conv_sT = jnp.swapaxes(state_conv, 1, 2)
    zero_buf8 = jnp.zeros((B, SUBLANES, DR), F32)
    zero_h = jnp.zeros((B, 1, DR), F32)

    xp = x_prompt.reshape(Mp, D)
    xs = jnp.pad(x_sample.reshape(DB, D), ((0, Ms - DB), (0, 0)))
    outs_p = [[] for _ in range(5)]
    outs_s = [[] for _ in range(5)]

    tn_ada = _pick_tile(6 * D, 0)
    streams_p, streams_s = [], []
    for l in range(L):
        (mod,) = _matmul(c_all, w_ada, l, 0, 6 * D, [(6 * D, F32)], lambda acc, b: [acc + b],
                         extras=[(b_ada3, pl.BlockSpec((None, 1, tn_ada), lambda j, i, l=l: (l, 0, j)))],
                         a_fn=_silu_bf16, name="ada")
        streams_p.append(_Stream(mod[:B].reshape(B, 1, 6 * D), T, Mp, D))
        streams_s.append(_Stream(mod[B:B + Ms], 1, Ms, D))
    streams_p.append(None)
    streams_s.append(None)

    hp = _prenorm(xp, norm_g3, 0, streams_p[0], 0, 1)
    hs = _prenorm(xs, norm_g3, 0, streams_s[0], 0, 1)
    for l in range(L):
        proj_p, proj_s = _in_projection(hp, hs, w_inT, l, dims, o_xr)

        q, kv, kvb, iq, ikw, ikb, xrg, gates_p = proj_p
        qT = jnp.swapaxes(q.reshape(B * nq, tq, att), 1, 2)
        iqT = jnp.swapaxes(iq.reshape(B * nq, tq, idw), 1, 2)
        vT = jnp.swapaxes(kvb[:, att:].reshape(B * nq, tq, att), 1, 2)
        wT = jnp.swapaxes(ikw[:, idx_dim:].reshape(B * nq, tq, LANES), 1, 2)
        yT = _dsa_prompt(qT, kvb, vT, iqT, ikb, wT, bias_tiles, B=B, T=T, tq=tq, topk=topk_p,
                         n_heads=n_heads, head_dim=head_dim, idx_heads=idx_heads, idx_dim=idx_dim)
        ya_p = jnp.swapaxes(yT, 1, 2).reshape(Mp, att)
        yr_p, h_last, nbuf = _rglru_seq(xrg, zero_buf8, zero_h, lw_all, l, B=B, T=T, DR=DR)
        outs_p[0].append(kv[:, :att].reshape(B, T, n_heads, head_dim))
        outs_p[1].append(kv[:, att:].reshape(B, T, n_heads, head_dim))
        outs_p[2].append(ikw[:, :idx_dim].reshape(B, T, idx_dim))
        outs_p[3].append(h_last.reshape(B, DR))
        outs_p[4].append(nbuf[:, SUBLANES - (CONV_W - 1):])

        q, kv, kvb, iq, ikw, ikb, xrg, gates_s = proj_s
        k_new = kv[:DB, :att]
        v_new = kv[:DB, att:]
        ik_new = ikw[:DB, :idx_dim]
        iq3 = iq[:DB].reshape(DB, idx_heads, idx_dim)
        w3 = jnp.broadcast_to(ikw[:DB, idx_dim:idx_dim + idx_heads, None], (DB, idx_heads, page))
        new_pages = jnp.zeros((DB, page, idx_dim), F32).at[:, 0].set(ik_new)
        scores = _sample_scores(page_table, iq3, w3, cache_idx_k, new_pages, l)
        idx = _sample_topk(scores.reshape(DB, -1), topk_s)
        q_f32 = q[:DB].astype(F32) / dims["q_scale"]
        ya_s = _sample_attend(idx, page_table, q_f32.reshape(DB, n_heads, head_dim),
                              k_new.reshape(DB, n_heads, head_dim), v_new.reshape(DB, n_heads, head_dim),
                              rel_pad, cache_k, cache_v, l)
        ya_s = jnp.pad(ya_s.reshape(DB, att), ((0, Ms - DB), (0, 0))).astype(BF16)
        y_s, h_new, nbufT = _rglru_step(xrg, conv_sT, state_rglru_h, lw_all, l, DB=DB, DR=DR)
        yr_s = jnp.pad(y_s, ((0, Ms - DB), (0, 0))).astype(BF16)
        outs_s[0].append(k_new.reshape(DB, 1, n_heads, head_dim))
        outs_s[1].append(v_new.reshape(DB, 1, n_heads, head_dim))
        outs_s[2].append(ik_new.reshape(DB, 1, idx_dim))
        outs_s[3].append(h_new)
        outs_s[4].append(jnp.swapaxes(nbufT, 0, 1))

        xp, hp, xs, hs = _finish_layer(xp, xs, (ya_p, yr_p, gates_p), (ya_s, yr_s, gates_s),
                                       streams_p[l], streams_s[l], streams_p[l + 1], streams_s[l + 1],
                                       lw_all, norm_g3, l)

    y_prompt = xp.reshape(B, T, D)
    y_sample = xs[:DB].reshape(DB, 1, D)
    return (y_prompt, y_sample, *[jnp.stack(o) for o in outs_p], *[jnp.stack(o) for o in outs_s])
```
